```python
import math
import jax
import jax.numpy as jnp
from jax import lax
import numpy as np

D_MODEL = 1024
BATCH = 8
SEQ = 4096
DEPTH = 2

HEAD_DIM = 64
NORM_EPS = 1e-6
ROPE_THETA = 10000.0

GDN_HEADS = 4
GDN_DK = 64
GDN_DV = 64
GDN_CHUNK = 64
CONV_K = 5

DIL_HEADS = 4
DIL_PATTERNS = ((128, 1), (512, 4), (2048, 16))

DIFF_HEADS = 4
DIFF_DIM = HEAD_DIM
Q_BLOCK = 128

D_FF = ((8 * D_MODEL + 3 * 256 - 1) // (3 * 256)) * 256

A_QK = GDN_HEADS * GDN_DK
A_V = GDN_HEADS * GDN_DV
A_GATE = 2 * GDN_HEADS
B_W = DIL_HEADS * HEAD_DIM
C_QK = DIFF_HEADS * 2 * DIFF_DIM
C_V = DIFF_HEADS * 2 * DIFF_DIM
IN_SPLITS = (A_QK, A_QK, A_V, A_V, A_GATE, A_GATE, B_W, B_W, B_W, C_QK, C_QK, C_V)
IN_WIDTH = sum(IN_SPLITS)
GDN_CONV_CH = 2 * A_QK + A_V
MIX_WIDTH = A_V + B_W + C_V
MAX_POS_OFFSET = 4096

kernel_name = 'hybrid_gdn_dilated_diff_encoder'


def rms_norm(x, w):
    xf = x.astype(jnp.float32)
    y = xf * lax.rsqrt(jnp.mean(xf * xf, axis=-1, keepdims=True) + NORM_EPS)
    return (y * w.astype(jnp.float32)).astype(x.dtype)


def l2norm(t):
    tf = t.astype(jnp.float32)
    return tf * lax.rsqrt(jnp.sum(tf * tf, axis=-1, keepdims=True) + 1e-6)


def rope_tables(positions, dim):
    inv = ROPE_THETA ** (-jnp.arange(0, dim, 2, dtype=jnp.float32) / dim)
    ang = positions.astype(jnp.float32)[..., None] * inv
    return jnp.cos(ang)[:, :, None, :], jnp.sin(ang)[:, :, None, :]


def apply_rope(x, cos, sin):
    x1, x2 = jnp.split(x.astype(jnp.float32), 2, axis=-1)
    return jnp.concatenate([x1 * cos - x2 * sin, x2 * cos + x1 * sin], axis=-1).astype(x.dtype)


def centred_depthwise_conv(x, w):
    pad = (CONV_K - 1) // 2
    return lax.conv_general_dilated(
        x, w[:, None, :].astype(x.dtype), window_strides=(1,), padding=[(pad, pad)],
        dimension_numbers=('NWC', 'WIO', 'NWC'), feature_group_count=x.shape[-1])


def chunk_gated_delta_rule(q, k, v, g, beta):
    f32 = jnp.float32
    N, H, S, dk = q.shape
    dv = v.shape[-1]
    C = GDN_CHUNK
    nc = S // C
    q = q.astype(f32) * (dk ** -0.5)
    k = k.astype(f32)
    v = v.astype(f32)
    beta = beta.astype(f32)
    q, k, v = (t.reshape(N, H, nc, C, t.shape[-1]) for t in (q, k, v))
    beta = beta.reshape(N, H, nc, C)
    g = jnp.cumsum(g.astype(f32).reshape(N, H, nc, C), axis=-1)
    tril = jnp.tril(jnp.ones((C, C), dtype=bool))
    strict = jnp.tril(jnp.ones((C, C), dtype=bool), -1)
    decay = jnp.exp(jnp.where(tril, g[..., :, None] - g[..., None, :], -jnp.inf))
    k_beta = k * beta[..., None]
    v_beta = v * beta[..., None]
    lower = jnp.where(strict, jnp.einsum('nhcik,nhcjk->nhcij', k_beta, k) * decay, 0.0)
    eye = jnp.eye(C, dtype=f32)
    t_inv = lax.linalg.triangular_solve(eye + lower, jnp.broadcast_to(eye, lower.shape),
                                        left_side=True, lower=True, unit_diagonal=True)
    u = jnp.einsum('nhcij,nhcjv->nhciv', t_inv, v_beta)
    k_cum = jnp.einsum('nhcij,nhcjk->nhcik', t_inv, k_beta * jnp.exp(g)[..., None])
    intra = jnp.where(tril, jnp.einsum('nhcik,nhcjk->nhcij', q, k) * decay, 0.0)
    xs = tuple(jnp.moveaxis(t, 2, 0) for t in (q, k, u, k_cum, intra, g))

    def step(state, inp):
        q_c, k_c, u_c, kc_c, a_c, g_c = inp
        v_new = u_c - jnp.einsum('nhck,nhkv->nhcv', kc_c, state)
        out = (jnp.einsum('nhck,nhkv->nhcv', q_c * jnp.exp(g_c)[..., None], state)
               + jnp.einsum('nhij,nhjv->nhiv', a_c, v_new))
        g_last = g_c[..., -1]
        state = (state * jnp.exp(g_last)[..., None, None]
                 + jnp.einsum('nhck,nhcv->nhkv', k_c * jnp.exp(g_last[..., None] - g_c)[..., None], v_new))
        return state, out

    state0 = jnp.zeros((N, H, dk, dv), f32)
    _, out = lax.scan(step, state0, xs)
    return jnp.moveaxis(out, 0, 2).reshape(N, H, S, dv)


def gdn_mixer(q, k, v, z, a, b, conv_w, a_log, dt_bias, norm_w):
    f32 = jnp.float32
    B, S, _ = q.shape
    qkv = jax.nn.silu(centred_depthwise_conv(jnp.concatenate([q, k, v], axis=-1), conv_w))
    q, k, v = jnp.split(qkv, [A_QK, 2 * A_QK], axis=-1)
    q = l2norm(q.reshape(B, S, GDN_HEADS, GDN_DK))
    k = l2norm(k.reshape(B, S, GDN_HEADS, GDN_DK))
    v = v.reshape(B, S, GDN_HEADS, GDN_DV)
    a = a.reshape(B, S, 2, GDN_HEADS).astype(f32)
    b = b.reshape(B, S, 2, GDN_HEADS).astype(f32)
    g = -jnp.exp(a_log.astype(f32)) * jax.nn.softplus(a + dt_bias.astype(f32))
    beta = jax.nn.sigmoid(b)

    def both(fwd, bwd):
        return jnp.concatenate([fwd, bwd[:, ::-1]], axis=0)

    qd = both(q, q).transpose(0, 2, 1, 3)
    kd = both(k, k).transpose(0, 2, 1, 3)
    vd = both(v, v).transpose(0, 2, 1, 3)
    gd = both(g[:, :, 0], g[:, :, 1]).transpose(0, 2, 1)
    bd = both(beta[:, :, 0], beta[:, :, 1]).transpose(0, 2, 1)
    o = chunk_gated_delta_rule(qd, kd, vd, gd, bd)
    o = (o[:B] + o[B:, :, ::-1]).transpose(0, 2, 1, 3)
    o = rms_norm(o, norm_w) * jax.nn.silu(z.reshape(B, S, GDN_HEADS, GDN_DV).astype(f32))
    return o.reshape(B, S, A_V)


def banded_attention(q, k, v, half):
    f32 = jnp.float32
    N, L, H, D = q.shape
    W = half
    nb = -(-L // W)
    pad = nb * W - L
    qb = jnp.pad(q, ((0, 0), (0, pad), (0, 0), (0, 0))).reshape(N, nb, W, H, D)
    kv_pad = ((0, 0), (W, pad + W), (0, 0), (0, 0))
    kp = jnp.pad(k, kv_pad).reshape(N, nb + 2, W, H, D)
    vp = jnp.pad(v, kv_pad).reshape(N, nb + 2, W, H, D)
    kb = jnp.concatenate([kp[:, :-2], kp[:, 1:-1], kp[:, 2:]], axis=2)
    vb = jnp.concatenate([vp[:, :-2], vp[:, 1:-1], vp[:, 2:]], axis=2)
    qpos = jnp.arange(nb * W).reshape(nb, W)
    kpos = jnp.arange(nb)[:, None] * W - W + jnp.arange(3 * W)[None, :]
    dist = kpos[:, None, :] - qpos[:, :, None]
    valid = (jnp.abs(dist) <= W) & (kpos[:, None, :] >= 0) & (kpos[:, None, :] < L)
    s = jnp.einsum('nbqhd,nbkhd->nbhqk', qb, kb).astype(f32) * (D ** -0.5)
    s = jnp.where(valid[None, :, None], s, -jnp.inf)
    m = jnp.max(s, axis=-1, keepdims=True)
    p = jnp.exp(s - m)
    den = jnp.sum(p, axis=-1, keepdims=True)
    o = jnp.einsum('nbhqk,nbkhd->nbqhd', p / den, vb.astype(f32)).reshape(N, nb * W, H, D)[:, :L]
    lse = (m + jnp.log(den))[..., 0].transpose(0, 1, 3, 2).reshape(N, nb * W, H)[:, :L]
    return o, lse


def dilated_mixer(q, k, v):
    B, S, H, D = q.shape
    outs, lses = [], []
    for window, dil in DIL_PATTERNS:
        half = window // (2 * dil)
        L = S // dil

        def by_residue(t):
            return t.reshape(B, L, dil, H, D).transpose(0, 2, 1, 3, 4).reshape(B * dil, L, H, D)

        o, lse = banded_attention(by_residue(q), by_residue(k), by_residue(v), half)
        outs.append(o.reshape(B, dil, L, H, D).transpose(0, 2, 1, 3, 4).reshape(B, S, H, D))
        lses.append(lse.reshape(B, dil, L, H).transpose(0, 2, 1, 3).reshape(B, S, H))
    weights = jax.nn.softmax(jnp.stack(lses, axis=0), axis=0)
    return jnp.einsum('pbsh,pbshd->bshd', weights, jnp.stack(outs, axis=0))


def diff_mixer(q, k, v, lambda_q1, lambda_k1, lambda_q2, lambda_k2, subln_w, lambda_init):
    f32 = jnp.float32
    B, S, H, _, Dc = q.shape
    lam = (jnp.exp(jnp.sum(lambda_q1.astype(f32) * lambda_k1.astype(f32)))
           - jnp.exp(jnp.sum(lambda_q2.astype(f32) * lambda_k2.astype(f32))) + lambda_init)
    nq = S // Q_BLOCK
    qb = q.reshape(B, nq, Q_BLOCK, H, 2, Dc).transpose(1, 0, 2, 3, 4, 5)
    vf = v.astype(f32)

    def block(q_blk):
        s = jnp.einsum('bqhcd,bkhcd->bhcqk', q_blk, k).astype(f32) * (Dc ** -0.5)
        p = jax.nn.softmax(s, axis=-1)
        attn = p[:, :, 0] - lam * p[:, :, 1]
        return jnp.einsum('bhqk,bkhd->bqhd', attn, vf)

    o = lax.map(block, qb)
    o = o.transpose(1, 0, 2, 3, 4).reshape(B, S, H, 2 * Dc)
    return rms_norm(o, subln_w) * (1.0 - lambda_init)


def hybrid_layer(x, cos, sin, attn_norm_w, w_in, conv_w, a_log, dt_bias, gdn_norm_w,
                 lambda_q1, lambda_k1, lambda_q2, lambda_k2, subln_w, w_out,
                 ffn_norm_w, w_gate, w_up, w_down, lambda_init):
    B, S, _ = x.shape
    h = rms_norm(x, attn_norm_w)
    split_at = [int(i) for i in np.cumsum(IN_SPLITS)[:-1]]
    aq, ak, av, az, aa, ab, bq, bk, bv, cq, ck, cv = jnp.split(h @ w_in, split_at, axis=-1)
    o_a = gdn_mixer(aq, ak, av, az, aa, ab, conv_w, a_log, dt_bias, gdn_norm_w)
    bq = apply_rope(bq.reshape(B, S, DIL_HEADS, HEAD_DIM), cos, sin)
    bk = apply_rope(bk.reshape(B, S, DIL_HEADS, HEAD_DIM), cos, sin)
    o_b = dilated_mixer(bq, bk, bv.reshape(B, S, DIL_HEADS, HEAD_DIM)).reshape(B, S, B_W)
    cq = apply_rope(cq.reshape(B, S, 2 * DIFF_HEADS, DIFF_DIM), cos, sin).reshape(B, S, DIFF_HEADS, 2, DIFF_DIM)
    ck = apply_rope(ck.reshape(B, S, 2 * DIFF_HEADS, DIFF_DIM), cos, sin).reshape(B, S, DIFF_HEADS, 2, DIFF_DIM)
    o_c = diff_mixer(cq, ck, cv.reshape(B, S, DIFF_HEADS, 2 * DIFF_DIM), lambda_q1, lambda_k1,
                     lambda_q2, lambda_k2, subln_w, lambda_init).reshape(B, S, C_V)
    mix = jnp.concatenate([o_a.astype(x.dtype), o_b.astype(x.dtype), o_c.astype(x.dtype)], axis=-1)
    x = x + mix @ w_out
    h = rms_norm(x, ffn_norm_w)
    return x + (jax.nn.silu(h @ w_gate) * (h @ w_up)) @ w_down


def setup_inputs(seed: int = 0) -> dict:
    key = jax.random.key(seed)
    ks = jax.random.split(key, 20)
    f32 = jnp.float32

    def nrm(k, shape, scale):
        return jax.random.normal(k, shape, f32) * scale

    x = jax.random.normal(ks[0], (BATCH, SEQ, D_MODEL), f32)
    positions = (jnp.arange(SEQ, dtype=jnp.int32)[None, :]
                 + jax.random.randint(ks[1], (BATCH, 1), 0, MAX_POS_OFFSET, dtype=jnp.int32))
    attn_norm_w = 1.0 + nrm(ks[2], (DEPTH, D_MODEL), 0.02)
    w_in = nrm(ks[3], (DEPTH, D_MODEL, IN_WIDTH), D_MODEL ** -0.5)
    conv_w = nrm(ks[4], (DEPTH, CONV_K, GDN_CONV_CH), CONV_K ** -0.5)
    a_log = jnp.log(jax.random.uniform(ks[5], (DEPTH, 2, GDN_HEADS), f32, 1.0, 16.0))
    dt = jnp.exp(jax.random.uniform(ks[6], (DEPTH, 2, GDN_HEADS), f32, math.log(1e-3), math.log(1e-1)))
    dt_bias = dt + jnp.log(-jnp.expm1(-dt))
    gdn_norm_w = 1.0 + nrm(ks[7], (DEPTH, GDN_DV), 0.02)
    lambda_q1 = nrm(ks[8], (DEPTH, DIFF_DIM), 0.1)
    lambda_k1 = nrm(ks[9], (DEPTH, DIFF_DIM), 0.1)
    lambda_q2 = nrm(ks[10], (DEPTH, DIFF_DIM), 0.1)
    lambda_k2 = nrm(ks[11], (DEPTH, DIFF_DIM), 0.1)
    subln_w = 1.0 + nrm(ks[12], (DEPTH, 2 * DIFF_DIM), 0.02)
    w_out = nrm(ks[13], (DEPTH, MIX_WIDTH, D_MODEL), MIX_WIDTH ** -0.5)
    ffn_norm_w = 1.0 + nrm(ks[14], (DEPTH, D_MODEL), 0.02)
    w_gate = nrm(ks[15], (DEPTH, D_MODEL, D_FF), D_MODEL ** -0.5)
    w_up = nrm(ks[16], (DEPTH, D_MODEL, D_FF), D_MODEL ** -0.5)
    w_down = nrm(ks[17], (DEPTH, D_FF, D_MODEL), D_FF ** -0.5)
    final_norm_w = 1.0 + nrm(ks[18], (D_MODEL,), 0.02)
    return {'x': x, 'positions': positions, 'attn_norm_w': attn_norm_w, 'w_in': w_in,
            'conv_w': conv_w, 'a_log': a_log, 'dt_bias': dt_bias, 'gdn_norm_w': gdn_norm_w,
            'lambda_q1': lambda_q1, 'lambda_k1': lambda_k1, 'lambda_q2': lambda_q2,
            'lambda_k2': lambda_k2, 'subln_w': subln_w, 'w_out': w_out,
            'ffn_norm_w': ffn_norm_w, 'w_gate': w_gate, 'w_up': w_up, 'w_down': w_down,
            'final_norm_w': final_norm_w}


def reference(x, positions, attn_norm_w, w_in, conv_w, a_log, dt_bias, gdn_norm_w,
              lambda_q1, lambda_k1, lambda_q2, lambda_k2, subln_w, w_out,
              ffn_norm_w, w_gate, w_up, w_down, final_norm_w):
    cos, sin = rope_tables(positions, HEAD_DIM)
    for l in range(DEPTH):
        lambda_init = 0.8 - 0.6 * math.exp(-0.3 * l)
        x = hybrid_layer(x, cos, sin, attn_norm_w[l], w_in[l], conv_w[l], a_log[l], dt_bias[l],
                         gdn_norm_w[l], lambda_q1[l], lambda_k1[l], lambda_q2[l], lambda_k2[l],
                         subln_w[l], w_out[l], ffn_norm_w[l], w_gate[l], w_up[l], w_down[l],
                         lambda_init)
    return rms_norm(x, final_norm_w)
```

```python
import functools
import math

import numpy as np
import jax
import jax.numpy as jnp
from jax import lax
from jax.experimental import pallas as pl
from jax.experimental.pallas import tpu as pltpu

F32 = jnp.float32
BF16 = jnp.bfloat16

NORM_EPS = 1e-6
ROPE_THETA = 10000.0
HEAD_DIM = 64
LANES = 128

GDN_HEADS = 4
GDN_CHUNK = 64
CONV_K = 5
CONV_HALO = 8

DIL_HEADS = 4
DIL_PATTERNS = ((128, 1), (512, 4), (2048, 16))
DIL_REACH = max(w // 2 for w, _ in DIL_PATTERNS)
DIL_TQ = 256

DIFF_HEADS = 4
DIFF_TQ = 256
DIFF_KC = 512

A_W = GDN_HEADS * HEAD_DIM
B_W = DIL_HEADS * HEAD_DIM
C_W = DIFF_HEADS * 2 * HEAD_DIM
GATE_W = 2 * 2 * GDN_HEADS
NEG_BIG = -1e30

VMEM_LIMIT = 56 * 1024 * 1024


def _cparams(*sem):
    return pltpu.CompilerParams(dimension_semantics=sem, vmem_limit_bytes=VMEM_LIMIT)


def _split3(x):
    hi = x.astype(BF16)
    r1 = x - hi.astype(F32)
    mid = r1.astype(BF16)
    lo = (r1 - mid.astype(F32)).astype(BF16)
    return hi, mid, lo


def _dot(a, b):
    return jnp.dot(a, b, preferred_element_type=F32)


def _dot_nt(a, b):
    return lax.dot_general(a, b, (((1,), (1,)), ((), ())), preferred_element_type=F32)


def _dot_tn(a, b):
    return lax.dot_general(a, b, (((0,), (0,)), ((), ())), preferred_element_type=F32)


def _dot_exact_rhs(x, m_bf16):
    hi, mid, lo = _split3(x)
    return _dot(hi, m_bf16) + _dot(mid, m_bf16) + _dot(lo, m_bf16)


def _dot_exact_lhs(m_bf16, x):
    hi, mid, lo = _split3(x)
    return _dot(m_bf16, hi) + _dot(m_bf16, mid) + _dot(m_bf16, lo)


def _sigmoid(x):
    return 1.0 / (1.0 + jnp.exp(-x))


def _silu(x):
    return x * _sigmoid(x)


def _rope_kernel(pos_ref, inv_ref, sign_ref, cos_ref, sin_ref):
    ang = pos_ref[...].astype(F32) * inv_ref[...]
    cos_ref[...] = jnp.cos(ang)
    sin_ref[...] = jnp.sin(ang) * sign_ref[...]


def _rope_tables(positions):
    t = positions.size
    tr = min(t, 1024)
    half = HEAD_DIM // 2
    inv = ROPE_THETA ** (-jnp.arange(0, HEAD_DIM, 2, dtype=F32) / HEAD_DIM)
    inv_row = jnp.tile(inv, LANES // half)[None, :]
    sign_row = jnp.asarray(np.where((np.arange(LANES) % HEAD_DIM) < half, -1.0, 1.0), F32)[None, :]
    row = pl.BlockSpec((1, LANES), lambda i: (0, 0))
    out = pl.BlockSpec((tr, LANES), lambda i: (i, 0))
    return pl.pallas_call(
        _rope_kernel,
        grid=(t // tr,),
        in_specs=[pl.BlockSpec((tr, 1), lambda i: (i, 0)), row, row],
        out_specs=[out, out],
        out_shape=[jax.ShapeDtypeStruct((t, LANES), F32)] * 2,
        compiler_params=_cparams("parallel"),
        name="rope_tables",
    )(positions.reshape(t, 1), inv_row, sign_row)


def _rope(y, cos, sin):
    half = HEAD_DIM // 2
    lane = lax.broadcasted_iota(jnp.int32, cos.shape, 1)
    first_half = (lane % HEAD_DIM) < half
    slabs = []
    for c0 in range(0, y.shape[1], LANES):
        ys = y[:, c0:c0 + LANES]
        partner = jnp.where(first_half, pltpu.roll(ys, LANES - half, 1), pltpu.roll(ys, half, 1))
        slabs.append(ys * cos + partner * sin)
    return jnp.concatenate(slabs, axis=1)


def _inproj_kernel(x_ref, nw_ref, w_ref, cos_ref, sin_ref,
                   aqkv_ref, az_ref, gate_ref, bqkv_ref, cq_ref, ck_ref, cv_ref):
    x = x_ref[...]
    h = x * lax.rsqrt(jnp.mean(x * x, axis=-1, keepdims=True) + NORM_EPS) * nw_ref[...]
    h = h.astype(BF16)
    cos = cos_ref[...]
    sin = sin_ref[...]
    scale = HEAD_DIM ** -0.5

    def proj(start, width):
        return _dot(h, w_ref[:, start:start + width])

    o = 0
    aqkv_ref[...] = proj(o, 3 * A_W)
    o += 3 * A_W
    az_ref[...] = proj(o, A_W)
    o += A_W
    gate_ref[...] = proj(o, LANES)
    o += LANES
    bqkv_ref[:, 0:B_W] = (_rope(proj(o, B_W), cos, sin) * scale).astype(BF16)
    o += B_W
    bqkv_ref[:, B_W:2 * B_W] = _rope(proj(o, B_W), cos, sin).astype(BF16)
    o += B_W
    bqkv_ref[:, 2 * B_W:3 * B_W] = proj(o, B_W).astype(BF16)
    o += B_W
    cq_ref[...] = (_rope(proj(o, C_W), cos, sin) * scale).astype(BF16)
    o += C_W
    ck_ref[...] = _rope(proj(o, C_W), cos, sin).astype(BF16)
    o += C_W
    cv_ref[...] = proj(o, C_W).astype(BF16)


def _norm_inproj(x2d, norm_w, w_pad, cos, sin, tm=512):
    t, d = x2d.shape
    tm = min(tm, t)
    np_ = w_pad.shape[1]
    rows = lambda width: pl.BlockSpec((tm, width), lambda i: (i, 0))
    out_w = (3 * A_W, A_W, LANES, 3 * B_W, C_W, C_W, C_W)
    out_dt = (F32, F32, F32, BF16, BF16, BF16, BF16)
    return pl.pallas_call(
        _inproj_kernel,
        grid=(t // tm,),
        in_specs=[rows(d), pl.BlockSpec((1, d), lambda i: (0, 0)),
                  pl.BlockSpec((d, np_), lambda i: (0, 0)), rows(LANES), rows(LANES)],
        out_specs=[rows(w) for w in out_w],
        out_shape=[jax.ShapeDtypeStruct((t, w), dt) for w, dt in zip(out_w, out_dt)],
        compiler_params=_cparams("parallel"),
        name="norm_inproj",
    )(x2d, norm_w.reshape(1, d), w_pad, cos, sin)


def _gdn_prep_kernel(prev_ref, cur_ref, next_ref, gate_ref, cw_ref, gp_ref, bd_ref,
                     qkv_ref, gact_ref, ext_ref):
    i = pl.program_id(1)
    n = pl.num_programs(1)
    tr = cur_ref.shape[0]
    ext_ref[0:CONV_HALO, :] = jnp.where(i > 0, prev_ref[...], 0.0)
    ext_ref[CONV_HALO:CONV_HALO + tr, :] = cur_ref[...]
    ext_ref[CONV_HALO + tr:, :] = jnp.where(i < n - 1, next_ref[...], 0.0)
    pad = (CONV_K - 1) // 2
    acc = None
    for j in range(CONV_K):
        term = ext_ref[pl.ds(CONV_HALO - pad + j, tr), :] * cw_ref[j:j + 1, :]
        acc = term if acc is None else acc + term
    y = _silu(acc)
    bd = bd_ref[...]
    dk_scale = HEAD_DIM ** -0.5
    for part, mul in ((0, dk_scale), (1, 1.0)):
        t = y[:, part * A_W:(part + 1) * A_W]
        ss = _dot_exact_rhs(t * t, bd)
        qkv_ref[:, part * A_W:(part + 1) * A_W] = t * (lax.rsqrt(ss + 1e-6) * mul)
    qkv_ref[:, 2 * A_W:] = y[:, 2 * A_W:]
    a = gate_ref[...]
    z = a + gp_ref[1:2, :]
    softplus = jnp.maximum(z, 0.0) + jnp.log(1.0 + jnp.exp(-jnp.abs(z)))
    g = gp_ref[0:1, :] * softplus
    lane = lax.broadcasted_iota(jnp.int32, a.shape, 1)
    gact_ref[...] = jnp.where(lane < GATE_W // 2, g, jnp.where(lane < GATE_W, _sigmoid(a), 0.0))


def _gdn_prep(aqkv, gates, conv_w, a_log, dt_bias, tr=512):
    b, s, w = aqkv.shape
    tr = min(tr, s)
    hb = tr // CONV_HALO
    nblk8 = s // CONV_HALO
    cw = jnp.zeros((8, w), F32).at[:CONV_K].set(conv_w.astype(F32))
    gp = jnp.zeros((8, LANES), F32)
    gp = gp.at[0, :GATE_W // 2].set(-jnp.exp(a_log.astype(F32).reshape(-1)))
    gp = gp.at[1, :GATE_W // 2].set(dt_bias.astype(F32).reshape(-1))
    head = np.arange(A_W) // HEAD_DIM
    bd = jnp.asarray(head[:, None] == head[None, :], BF16)
    const = lambda shape: pl.BlockSpec(shape, lambda bi, i: (0, 0))
    return pl.pallas_call(
        _gdn_prep_kernel,
        grid=(b, s // tr),
        in_specs=[
            pl.BlockSpec((None, CONV_HALO, w), lambda bi, i: (bi, jnp.maximum(i * hb - 1, 0), 0)),
            pl.BlockSpec((None, tr, w), lambda bi, i: (bi, i, 0)),
            pl.BlockSpec((None, CONV_HALO, w), lambda bi, i: (bi, jnp.minimum((i + 1) * hb, nblk8 - 1), 0)),
            pl.BlockSpec((None, tr, LANES), lambda bi, i: (bi, i, 0)),
            const((8, w)), const((8, LANES)), const((A_W, A_W)),
        ],
        out_specs=[pl.BlockSpec((None, tr, w), lambda bi, i: (bi, i, 0)),
                   pl.BlockSpec((None, tr, LANES), lambda bi, i: (bi, i, 0))],
        out_shape=[jax.ShapeDtypeStruct((b, s, w), F32), jax.ShapeDtypeStruct((b, s, LANES), F32)],
        scratch_shapes=[pltpu.VMEM((tr + 2 * CONV_HALO, w), F32)],
        compiler_params=_cparams("parallel", "parallel"),
        name="gdn_prep",
    )(aqkv, aqkv, aqkv, gates, cw, gp, bd)


def _gdn_consts():
    c = GDN_CHUNK
    w = A_W
    i = np.arange(c)[:, None]
    j = np.arange(w)[None, :] % c
    col_head = np.arange(w)[None, :] // c
    row = np.arange(w)[:, None]
    consts = {}
    for name, rev in (("f", False), ("r", True)):
        ge = (i <= j) if rev else (i >= j)
        consts["tril_" + name] = ge.astype(np.float32)
        consts["strict_" + name] = (ge & (i != j)).astype(np.float32)
        t = np.arange(c)
        tri = (t[None, :] >= t[:, None]) if rev else (t[None, :] <= t[:, None])
        consts["cum_" + name] = np.concatenate([tri.astype(np.float32), np.ones((c, c), np.float32)], axis=0)
        consts["upper_" + name] = ((i >= j) if rev else (i <= j)).astype(np.float32)
    blk16 = (i // 16) == (j // 16)
    blk32 = (i // 32) == (j // 32)
    consts["m16"] = blk16.astype(np.float32)
    consts["m32"] = (blk32 & ~blk16).astype(np.float32)
    consts["m64"] = (~blk32).astype(np.float32)
    consts["eye"] = (i == j).astype(np.float32)
    consts["bd"] = ((row // c) == col_head).astype(np.float32)
    sel = np.zeros((LANES, 4 * w), np.float32)
    for blk in range(4):
        for h in range(GDN_HEADS):
            sel[blk * GDN_HEADS + h, blk * w + h * c: blk * w + (h + 1) * c] = 1.0
    consts["sel"] = sel
    return consts


_GDN_CONST_ORDER = ("tril_f", "strict_f", "cum_f", "upper_f", "tril_r", "strict_r", "cum_r", "upper_r",
                    "m16", "m32", "m64", "eye", "bd", "sel")
_GDN_BF16_CONSTS = ("cum_f", "cum_r", "sel")


def _expand(wide, bd):
    return (jnp.concatenate([wide] * GDN_HEADS, axis=0) * bd).astype(BF16)


def _gdn_direction(qkv, gb, bb, cst, name, s_ref):
    c = GDN_CHUNK
    w = A_W
    bd = cst["bd"]
    q = qkv[:, 0:w]
    k = qkv[:, w:2 * w]
    v = qkv[:, 2 * w:3 * w]
    cum = _dot_exact_lhs(cst["cum_" + name], gb)
    gc = cum[0:c]
    gl = cum[c:2 * c]
    ones = cst["cum_" + name][c:2 * c]
    gr = _dot_exact_lhs(ones, gb * cst["upper_" + name])
    decay = jnp.exp(jnp.minimum(gc - gr, 0.0))
    kb = k * bb
    vb = v * bb
    eg = jnp.exp(gc)
    kbg = kb * eg
    qg = q * eg
    kd = k * jnp.exp(gl - gc)
    k_bd = _expand(k, bd)
    kq = _dot_nt(jnp.concatenate([kb, q], axis=0).astype(BF16), k_bd)
    low = kq[0:c] * decay * cst["strict_" + name]
    intra = kq[c:2 * c] * decay * cst["tril_" + name]

    eye = cst["eye"]
    n1 = -(low * cst["m16"])
    n2 = _dot(n1.astype(BF16), _expand(n1, bd))
    p0 = eye + n1
    r = _dot(jnp.concatenate([n2, p0], axis=0).astype(BF16), _expand(n2, bd))
    n4 = r[0:c]
    p1 = p0 + r[c:2 * c]
    r = _dot(jnp.concatenate([n4, p1], axis=0).astype(BF16), _expand(n4, bd))
    n8 = r[0:c]
    p2 = p1 + r[c:2 * c]
    inv = p2 + _dot(p2.astype(BF16), _expand(n8, bd))
    for mname in ("m32", "m64"):
        off = low * cst[mname]
        t1 = _dot(off.astype(BF16), _expand(inv, bd))
        inv = inv - _dot(inv.astype(BF16), _expand(t1, bd))

    rhs = jnp.concatenate([_expand(vb, bd), _expand(kbg, bd)], axis=1)
    uk = _dot(inv.astype(BF16), rhs)
    u = uk[:, 0:w]
    kcum = uk[:, w:2 * w]

    state = s_ref[...]
    ks = _dot(jnp.concatenate([kcum, qg], axis=0).astype(BF16), state.astype(BF16))
    v_new = u - ks[0:c]
    out = ks[c:2 * c] + _dot(intra.astype(BF16), _expand(v_new, bd))
    s_ref[...] = state * jnp.exp(gl[0:1, :]) + _dot_tn(kd.astype(BF16), v_new.astype(BF16)) * bd
    return out


def _gdn_scan_kernel(*refs):
    nconst = len(_GDN_CONST_ORDER)
    qkv_f_ref, gact_f_ref, qkv_r_ref, gact_r_ref = refs[0:4]
    cst = {n: r[...] for n, r in zip(_GDN_CONST_ORDER, refs[4:4 + nconst])}
    of_ref, or_ref, sf_ref, sr_ref = refs[4 + nconst:]

    @pl.when(pl.program_id(1) == 0)
    def _():
        sf_ref[...] = jnp.zeros_like(sf_ref)
        sr_ref[...] = jnp.zeros_like(sr_ref)

    w = A_W
    gf = _dot_exact_rhs(gact_f_ref[...], cst["sel"])
    of_ref[...] = _gdn_direction(qkv_f_ref[...], gf[:, 0:w], gf[:, 2 * w:3 * w], cst, "f", sf_ref)
    gr = _dot_exact_rhs(gact_r_ref[...], cst["sel"])
    or_ref[...] = _gdn_direction(qkv_r_ref[...], gr[:, w:2 * w], gr[:, 3 * w:4 * w], cst, "r", sr_ref)


def _gdn_scan(qkv, gact):
    b, s, w3 = qkv.shape
    c = GDN_CHUNK
    nc = s // c
    consts = _gdn_consts()
    const_arrays = [jnp.asarray(consts[n], BF16 if n in _GDN_BF16_CONSTS else F32) for n in _GDN_CONST_ORDER]
    fwd = lambda bi, ci: (bi, ci, 0)
    rev = lambda bi, ci: (bi, nc - 1 - ci, 0)
    in_specs = [pl.BlockSpec((None, c, w3), fwd), pl.BlockSpec((None, c, LANES), fwd),
                pl.BlockSpec((None, c, w3), rev), pl.BlockSpec((None, c, LANES), rev)]
    in_specs += [pl.BlockSpec(a.shape, lambda bi, ci: (0, 0)) for a in const_arrays]
    return pl.pallas_call(
        _gdn_scan_kernel,
        grid=(b, nc),
        in_specs=in_specs,
        out_specs=[pl.BlockSpec((None, c, A_W), fwd), pl.BlockSpec((None, c, A_W), rev)],
        out_shape=[jax.ShapeDtypeStruct((b, s, A_W), F32)] * 2,
        scratch_shapes=[pltpu.VMEM((A_W, A_W), F32)] * 2,
        compiler_params=_cparams("parallel", "arbitrary"),
        name="gdn_scan",
    )(qkv, gact, qkv, gact, *const_arrays)


def _dilated_bias(tq):
    r = DIL_REACH
    ii = np.arange(tq)[:, None]
    m = np.arange(tq + 4 * r)[None, :]
    d = 2 * r + ii - m
    count = np.zeros(d.shape, np.float64)
    for window, dil in DIL_PATTERNS:
        count += ((d % dil) == 0) & (np.abs(d) <= window // 2)
    bias = np.where(count > 0, np.log(np.maximum(count, 1.0)), NEG_BIG).astype(np.float32)
    return jnp.asarray(bias.reshape(tq, -1, tq).transpose(1, 0, 2))


def _dilated_kernel(q_ref, k_ref, v_ref, bias_ref, o_ref):
    tq = q_ref.shape[0]
    s = k_ref.shape[0]
    nwin = 1 + 2 * DIL_REACH // tq
    wl = nwin * tq
    q0 = pl.program_id(2) * tq
    ws = jnp.clip(q0 - DIL_REACH, 0, s - wl)
    mb0 = (2 * DIL_REACH - (q0 - ws)) // tq
    ws = pl.multiple_of(ws, tq)
    q = q_ref[...]
    kw = k_ref[pl.ds(ws, wl), :]
    vw = v_ref[pl.ds(ws, wl), :]
    bias = jnp.concatenate([bias_ref[mb0 + t] for t in range(nwin)], axis=1)
    lane = lax.broadcasted_iota(jnp.int32, q.shape, 1)
    outs = []
    for h in range(2):
        in_head = (lane // HEAD_DIM) == h
        sc = _dot_nt(jnp.where(in_head, q, jnp.zeros_like(q)), kw) + bias
        mx = jnp.max(sc, axis=-1, keepdims=True)
        e = jnp.exp(sc - mx)
        den = jnp.sum(e, axis=-1, keepdims=True)
        outs.append(_dot(e.astype(BF16), vw) / den)
    o_ref[...] = jnp.where(lane < HEAD_DIM, outs[0], outs[1]).astype(o_ref.dtype)


def _dilated(bqkv):
    b, s, _ = bqkv.shape
    tq = DIL_TQ
    npair = B_W // LANES
    bias = _dilated_bias(tq)
    return pl.pallas_call(
        _dilated_kernel,
        grid=(b, npair, s // tq),
        in_specs=[pl.BlockSpec((None, tq, LANES), lambda bi, p, i: (bi, i, p)),
                  pl.BlockSpec((None, s, LANES), lambda bi, p, i: (bi, 0, npair + p)),
                  pl.BlockSpec((None, s, LANES), lambda bi, p, i: (bi, 0, 2 * npair + p)),
                  pl.BlockSpec(bias.shape, lambda bi, p, i: (0, 0, 0))],
        out_specs=pl.BlockSpec((None, tq, LANES), lambda bi, p, i: (bi, i, p)),
        out_shape=jax.ShapeDtypeStruct((b, s, B_W), BF16),
        compiler_params=_cparams("parallel", "parallel", "arbitrary"),
        name="dilated_attn",
    )(bqkv, bqkv, bqkv, bias)


def _diff_kernel(lam_ref, q_ref, k_ref, v_ref, sw_ref, o_ref, *, lambda_init, kc):
    tq = q_ref.shape[0]
    s = k_ref.shape[0]
    lp = lam_ref[...]
    lam = (jnp.exp(jnp.sum(lp[0:1] * lp[1:2], axis=-1, keepdims=True))
           - jnp.exp(jnp.sum(lp[2:3] * lp[3:4], axis=-1, keepdims=True)) + lambda_init)
    q = q_ref[...]
    lane = lax.broadcasted_iota(jnp.int32, q.shape, 1)
    zero = jnp.zeros_like(q)
    qs = (jnp.where(lane < HEAD_DIM, q, zero), jnp.where(lane >= HEAD_DIM, q, zero))

    def body(j, carry):
        start = pl.multiple_of(j * kc, kc)
        kj = k_ref[pl.ds(start, kc), :]
        vj = v_ref[pl.ds(start, kc), :]
        new = []
        for comp in range(2):
            m_old, l_old, acc_old = carry[comp]
            sc = _dot_nt(qs[comp], kj)
            m_new = jnp.maximum(m_old, jnp.max(sc, axis=-1, keepdims=True))
            alpha = jnp.exp(m_old - m_new)
            e = jnp.exp(sc - m_new)
            l_new = alpha * l_old + jnp.sum(e, axis=-1, keepdims=True)
            acc_new = alpha * acc_old + _dot(e.astype(BF16), vj)
            new.append((m_new, l_new, acc_new))
        return tuple(new)

    init = tuple((jnp.full((tq, 1), NEG_BIG, F32), jnp.zeros((tq, 1), F32), jnp.zeros((tq, LANES), F32))
                 for _ in range(2))
    (_, l1, a1), (_, l2, a2) = lax.fori_loop(0, s // kc, body, init)
    o = a1 / l1 - lam * (a2 / l2)
    o = o * lax.rsqrt(jnp.mean(o * o, axis=-1, keepdims=True) + NORM_EPS) * sw_ref[...]
    o_ref[...] = (o * (1.0 - lambda_init)).astype(o_ref.dtype)


def _diff_attn(cq, ck, cv, lam_params, subln_w, lambda_init):
    b, s, _ = cq.shape
    tq = min(DIFF_TQ, s)
    kc = min(DIFF_KC, s)
    qspec = pl.BlockSpec((None, tq, LANES), lambda bi, h, i: (bi, i, h))
    kvspec = pl.BlockSpec((None, s, LANES), lambda bi, h, i: (bi, 0, h))
    return pl.pallas_call(
        functools.partial(_diff_kernel, lambda_init=lambda_init, kc=kc),
        grid=(b, DIFF_HEADS, s // tq),
        in_specs=[pl.BlockSpec((8, LANES), lambda bi, h, i: (0, 0)), qspec, kvspec, kvspec,
                  pl.BlockSpec((1, LANES), lambda bi, h, i: (0, 0))],
        out_specs=qspec,
        out_shape=jax.ShapeDtypeStruct((b, s, C_W), BF16),
        compiler_params=_cparams("parallel", "parallel", "arbitrary"),
        name="diff_attn",
    )(lam_params, cq, ck, cv, subln_w.reshape(1, LANES).astype(F32))


def _outproj_kernel(x_ref, of_ref, or_ref, z_ref, ob_ref, oc_ref, gw_ref, bd_ref, w_ref, y_ref):
    o = of_ref[...] + or_ref[...]
    ms = _dot_exact_rhs(o * o, bd_ref[...]) * (1.0 / HEAD_DIM)
    oa = o * lax.rsqrt(ms + NORM_EPS) * gw_ref[...] * _silu(z_ref[...])
    acc = x_ref[...] + _dot(oa.astype(BF16), w_ref[0:A_W, :])
    acc += _dot(ob_ref[...], w_ref[A_W:A_W + B_W, :])
    acc += _dot(oc_ref[...], w_ref[A_W + B_W:, :])
    y_ref[...] = acc


def _outproj(x2d, o_f, o_r, z, o_b, o_c, gdn_norm_w, w_out, tm=512):
    t, d = x2d.shape
    tm = min(tm, t)
    rows = lambda width: pl.BlockSpec((tm, width), lambda i: (i, 0))
    const = lambda shape: pl.BlockSpec(shape, lambda i: (0, 0))
    head = np.arange(A_W) // HEAD_DIM
    bd = jnp.asarray(head[:, None] == head[None, :], BF16)
    gw = jnp.tile(gdn_norm_w.astype(F32), GDN_HEADS)[None, :]
    return pl.pallas_call(
        _outproj_kernel,
        grid=(t // tm,),
        in_specs=[rows(d), rows(A_W), rows(A_W), rows(A_W), rows(B_W), rows(C_W),
                  const((1, A_W)), const((A_W, A_W)), const(w_out.shape)],
        out_specs=rows(d),
        out_shape=jax.ShapeDtypeStruct((t, d), F32),
        compiler_params=_cparams("parallel"),
        name="outproj",
    )(x2d, o_f, o_r, z, o_b, o_c, gw, bd, w_out)


def _ffn_kernel(x_ref, nw_ref, wg_ref, wu_ref, wd_ref, fw_ref, y_ref, *, fc, final_norm):
    x = x_ref[...]
    h = (x * lax.rsqrt(jnp.mean(x * x, axis=-1, keepdims=True) + NORM_EPS) * nw_ref[...]).astype(BF16)
    acc = x
    for c0 in range(0, wg_ref.shape[1], fc):
        g = _dot(h, wg_ref[:, c0:c0 + fc])
        u = _dot(h, wu_ref[:, c0:c0 + fc])
        acc = acc + _dot((_silu(g) * u).astype(BF16), wd_ref[c0:c0 + fc, :])
    if final_norm:
        acc = acc * lax.rsqrt(jnp.mean(acc * acc, axis=-1, keepdims=True) + NORM_EPS) * fw_ref[...]
    y_ref[...] = acc


def _ffn(x2d, norm_w, wg, wu, wd, final_w, final_norm, tm=512, fc=256):
    t, d = x2d.shape
    tm = min(tm, t)
    f = wg.shape[1]
    rows = pl.BlockSpec((tm, d), lambda i: (i, 0))
    const = lambda shape: pl.BlockSpec(shape, lambda i: (0, 0), pipeline_mode=pl.Buffered(1))
    return pl.pallas_call(
        functools.partial(_ffn_kernel, fc=fc, final_norm=final_norm),
        grid=(t // tm,),
        in_specs=[rows, const((1, d)), const((d, f)), const((d, f)), const((f, d)), const((1, d))],
        out_specs=rows,
        out_shape=jax.ShapeDtypeStruct((t, d), F32),
        compiler_params=_cparams("parallel"),
        name="ffn",
    )(x2d, norm_w.reshape(1, d).astype(F32), wg, wu, wd, final_w.reshape(1, d).astype(F32))


def _pad_in_weight(w_in):
    gate_end = 4 * A_W + GATE_W
    d = w_in.shape[0]
    return jnp.concatenate([w_in[:, :gate_end], jnp.zeros((d, LANES - GATE_W), w_in.dtype),
                            w_in[:, gate_end:]], axis=1).astype(BF16)


def _layer(x2d, b, s, cos, sin, attn_norm_w, w_in, conv_w, a_log, dt_bias, gdn_norm_w,
           lq1, lk1, lq2, lk2, subln_w, w_out, ffn_norm_w, w_gate, w_up, w_down,
           final_w, lambda_init, last):
    aqkv, az, gates, bqkv, cq, ck, cv = _norm_inproj(x2d, attn_norm_w.astype(F32), _pad_in_weight(w_in), cos, sin)
    qkv, gact = _gdn_prep(aqkv.reshape(b, s, -1), gates.reshape(b, s, -1), conv_w, a_log, dt_bias)
    o_f, o_r = _gdn_scan(qkv, gact)
    o_b = _dilated(bqkv.reshape(b, s, -1))
    lam_params = jnp.zeros((8, LANES), F32).at[0:4, :HEAD_DIM].set(
        jnp.stack([lq1, lk1, lq2, lk2]).astype(F32))
    o_c = _diff_attn(cq.reshape(b, s, -1), ck.reshape(b, s, -1), cv.reshape(b, s, -1),
                     lam_params, subln_w, lambda_init)
    t = b * s
    x2d = _outproj(x2d, o_f.reshape(t, -1), o_r.reshape(t, -1), az, o_b.reshape(t, -1),
                   o_c.reshape(t, -1), gdn_norm_w, w_out.astype(BF16))
    return _ffn(x2d, ffn_norm_w, w_gate.astype(BF16), w_up.astype(BF16), w_down.astype(BF16),
                final_w, last)


def kernel(x, positions, attn_norm_w, w_in, conv_w, a_log, dt_bias, gdn_norm_w, lambda_q1, lambda_k1,
           lambda_q2, lambda_k2, subln_w, w_out, ffn_norm_w, w_gate, w_up, w_down, final_norm_w):
    b, s, d = x.shape
    depth = w_in.shape[0]
    cos, sin = _rope_tables(positions)
    x2d = x.reshape(b * s, d)
    for l in range(depth):
        lambda_init = 0.8 - 0.6 * math.exp(-0.3 * l)
        x2d = _layer(x2d, b, s, cos, sin, attn_norm_w[l], w_in[l], conv_w[l], a_log[l], dt_bias[l],
                     gdn_norm_w[l], lambda_q1[l], lambda_k1[l], lambda_q2[l], lambda_k2[l], subln_w[l],
                     w_out[l], ffn_norm_w[l], w_gate[l], w_up[l], w_down[l], final_norm_w,
                     lambda_init, l == depth - 1)
    return x2d.reshape(b, s, d)
```

```python
import functools
import math

import numpy as np
import jax
import jax.numpy as jnp
from jax import lax
from jax.experimental import pallas as pl
from jax.experimental.pallas import tpu as pltpu

F32 = jnp.float32
BF16 = jnp.bfloat16

NORM_EPS = 1e-6
ROPE_THETA = 10000.0
HEAD_DIM = 64
LANES = 128

GDN_HEADS = 4
GDN_CHUNK = 64
GDN_GROUP = 4
CONV_K = 5
CONV_HALO = 8

DIL_HEADS = 4
DIL_PATTERNS = ((128, 1), (512, 4), (2048, 16))
DIL_REACH = max(w // 2 for w, _ in DIL_PATTERNS)
DIL_TQ = 256

DIFF_HEADS = 4
DIFF_TQ = 256
DIFF_KC = 512
DIFF_UNROLL = 4

A_W = GDN_HEADS * HEAD_DIM
B_W = DIL_HEADS * HEAD_DIM
C_W = DIFF_HEADS * 2 * HEAD_DIM
GATE_W = 2 * 2 * GDN_HEADS
NEG_BIG = -1e30
LOG2E = math.log2(math.e)

VMEM_LIMIT = 56 * 1024 * 1024


def _cparams(*sem):
    return pltpu.CompilerParams(dimension_semantics=sem, vmem_limit_bytes=VMEM_LIMIT)


def _split3(x):
    hi = x.astype(BF16)
    r1 = x - hi.astype(F32)
    mid = r1.astype(BF16)
    lo = (r1 - mid.astype(F32)).astype(BF16)
    return hi, mid, lo


def _dot(a, b):
    return jnp.dot(a, b, preferred_element_type=F32)


def _dot_nt(a, b):
    return lax.dot_general(a, b, (((1,), (1,)), ((), ())), preferred_element_type=F32)


def _dot_tn(a, b):
    return lax.dot_general(a, b, (((0,), (0,)), ((), ())), preferred_element_type=F32)


def _dot_exact_rhs(x, m_bf16):
    hi, mid, lo = _split3(x)
    return _dot(hi, m_bf16) + _dot(mid, m_bf16) + _dot(lo, m_bf16)


def _dot_exact_lhs(m_bf16, x):
    hi, mid, lo = _split3(x)
    return _dot(m_bf16, hi) + _dot(m_bf16, mid) + _dot(m_bf16, lo)


def _sigmoid(x):
    return 1.0 / (1.0 + jnp.exp(-x))


def _silu(x):
    return x * _sigmoid(x)


def _rope_kernel(pos_ref, inv_ref, sign_ref, cos_ref, sin_ref):
    ang = pos_ref[...].astype(F32) * inv_ref[...]
    cos_ref[...] = jnp.cos(ang)
    sin_ref[...] = jnp.sin(ang) * sign_ref[...]


def _rope_tables(positions):
    t = positions.size
    tr = min(t, 1024)
    half = HEAD_DIM // 2
    inv = ROPE_THETA ** (-jnp.arange(0, HEAD_DIM, 2, dtype=F32) / HEAD_DIM)
    inv_row = jnp.tile(inv, LANES // half)[None, :]
    sign_row = jnp.asarray(np.where((np.arange(LANES) % HEAD_DIM) < half, -1.0, 1.0), F32)[None, :]
    row = pl.BlockSpec((1, LANES), lambda i: (0, 0))
    out = pl.BlockSpec((tr, LANES), lambda i: (i, 0))
    return pl.pallas_call(
        _rope_kernel,
        grid=(t // tr,),
        in_specs=[pl.BlockSpec((tr, 1), lambda i: (i, 0)), row, row],
        out_specs=[out, out],
        out_shape=[jax.ShapeDtypeStruct((t, LANES), F32)] * 2,
        compiler_params=_cparams("parallel"),
        name="rope_tables",
    )(positions.reshape(t, 1), inv_row, sign_row)


def _rope(y, cos, sin):
    half = HEAD_DIM // 2
    lane = lax.broadcasted_iota(jnp.int32, cos.shape, 1)
    first_half = (lane % HEAD_DIM) < half
    slabs = []
    for c0 in range(0, y.shape[1], LANES):
        ys = y[:, c0:c0 + LANES]
        partner = jnp.where(first_half, pltpu.roll(ys, LANES - half, 1), pltpu.roll(ys, half, 1))
        slabs.append(ys * cos + partner * sin)
    return jnp.concatenate(slabs, axis=1)


def _inproj_kernel(x_ref, nw_ref, w_ref, cos_ref, sin_ref,
                   aqkv_ref, az_ref, gate_ref, bqkv_ref, cq_ref, ck_ref, cv_ref):
    x = x_ref[...]
    h = x * lax.rsqrt(jnp.mean(x * x, axis=-1, keepdims=True) + NORM_EPS) * nw_ref[...]
    h = h.astype(BF16)
    cos = cos_ref[...]
    sin = sin_ref[...]
    scale = HEAD_DIM ** -0.5

    def proj(start, width):
        return _dot(h, w_ref[:, start:start + width])

    o = 0
    aqkv_ref[...] = proj(o, 3 * A_W)
    o += 3 * A_W
    az_ref[...] = proj(o, A_W)
    o += A_W
    gate_ref[...] = proj(o, LANES)
    o += LANES
    bqkv_ref[:, 0:B_W] = (_rope(proj(o, B_W), cos, sin) * scale).astype(BF16)
    o += B_W
    bqkv_ref[:, B_W:2 * B_W] = _rope(proj(o, B_W), cos, sin).astype(BF16)
    o += B_W
    bqkv_ref[:, 2 * B_W:3 * B_W] = proj(o, B_W).astype(BF16)
    o += B_W
    cq_ref[...] = (_rope(proj(o, C_W), cos, sin) * (scale * LOG2E)).astype(BF16)
    o += C_W
    ck_ref[...] = _rope(proj(o, C_W), cos, sin).astype(BF16)
    o += C_W
    cv_ref[...] = proj(o, C_W).astype(BF16)


def _norm_inproj(x2d, norm_w, w_pad, cos, sin, tm=512):
    t, d = x2d.shape
    tm = min(tm, t)
    np_ = w_pad.shape[1]
    rows = lambda width: pl.BlockSpec((tm, width), lambda i: (i, 0))
    out_w = (3 * A_W, A_W, LANES, 3 * B_W, C_W, C_W, C_W)
    out_dt = (F32, F32, F32, BF16, BF16, BF16, BF16)
    return pl.pallas_call(
        _inproj_kernel,
        grid=(t // tm,),
        in_specs=[rows(d), pl.BlockSpec((1, d), lambda i: (0, 0)),
                  pl.BlockSpec((d, np_), lambda i: (0, 0)), rows(LANES), rows(LANES)],
        out_specs=[rows(w) for w in out_w],
        out_shape=[jax.ShapeDtypeStruct((t, w), dt) for w, dt in zip(out_w, out_dt)],
        compiler_params=_cparams("parallel"),
        name="norm_inproj",
    )(x2d, norm_w.reshape(1, d), w_pad, cos, sin)


def _gdn_prep_kernel(prev_ref, cur_ref, next_ref, gate_ref, cw_ref, gp_ref, bd_ref,
                     qkv_ref, gact_ref, ext_ref):
    i = pl.program_id(1)
    n = pl.num_programs(1)
    tr = cur_ref.shape[0]
    ext_ref[0:CONV_HALO, :] = jnp.where(i > 0, prev_ref[...], 0.0)
    ext_ref[CONV_HALO:CONV_HALO + tr, :] = cur_ref[...]
    ext_ref[CONV_HALO + tr:, :] = jnp.where(i < n - 1, next_ref[...], 0.0)
    pad = (CONV_K - 1) // 2
    acc = None
    for j in range(CONV_K):
        term = ext_ref[pl.ds(CONV_HALO - pad + j, tr), :] * cw_ref[j:j + 1, :]
        acc = term if acc is None else acc + term
    y = _silu(acc)
    bd = bd_ref[...]
    dk_scale = HEAD_DIM ** -0.5
    for part, mul in ((0, dk_scale), (1, 1.0)):
        t = y[:, part * A_W:(part + 1) * A_W]
        ss = _dot_exact_rhs(t * t, bd)
        qkv_ref[:, part * A_W:(part + 1) * A_W] = t * (lax.rsqrt(ss + 1e-6) * mul)
    qkv_ref[:, 2 * A_W:] = y[:, 2 * A_W:]
    a = gate_ref[...]
    z = a + gp_ref[1:2, :]
    softplus = jnp.maximum(z, 0.0) + jnp.log(1.0 + jnp.exp(-jnp.abs(z)))
    g = gp_ref[0:1, :] * softplus
    lane = lax.broadcasted_iota(jnp.int32, a.shape, 1)
    gact_ref[...] = jnp.where(lane < GATE_W // 2, g, jnp.where(lane < GATE_W, _sigmoid(a), 0.0))


def _gdn_prep(aqkv, gates, conv_w, a_log, dt_bias, tr=512):
    b, s, w = aqkv.shape
    tr = min(tr, s)
    hb = tr // CONV_HALO
    nblk8 = s // CONV_HALO
    cw = jnp.zeros((8, w), F32).at[:CONV_K].set(conv_w.astype(F32))
    gp = jnp.zeros((8, LANES), F32)
    gp = gp.at[0, :GATE_W // 2].set(-jnp.exp(a_log.astype(F32).reshape(-1)))
    gp = gp.at[1, :GATE_W // 2].set(dt_bias.astype(F32).reshape(-1))
    head = np.arange(A_W) // HEAD_DIM
    bd = jnp.asarray(head[:, None] == head[None, :], BF16)
    const = lambda shape: pl.BlockSpec(shape, lambda bi, i: (0, 0))
    return pl.pallas_call(
        _gdn_prep_kernel,
        grid=(b, s // tr),
        in_specs=[
            pl.BlockSpec((None, CONV_HALO, w), lambda bi, i: (bi, jnp.maximum(i * hb - 1, 0), 0)),
            pl.BlockSpec((None, tr, w), lambda bi, i: (bi, i, 0)),
            pl.BlockSpec((None, CONV_HALO, w), lambda bi, i: (bi, jnp.minimum((i + 1) * hb, nblk8 - 1), 0)),
            pl.BlockSpec((None, tr, LANES), lambda bi, i: (bi, i, 0)),
            const((8, w)), const((8, LANES)), const((A_W, A_W)),
        ],
        out_specs=[pl.BlockSpec((None, tr, w), lambda bi, i: (bi, i, 0)),
                   pl.BlockSpec((None, tr, LANES), lambda bi, i: (bi, i, 0))],
        out_shape=[jax.ShapeDtypeStruct((b, s, w), F32), jax.ShapeDtypeStruct((b, s, LANES), F32)],
        scratch_shapes=[pltpu.VMEM((tr + 2 * CONV_HALO, w), F32)],
        compiler_params=_cparams("parallel", "parallel"),
        name="gdn_prep",
    )(aqkv, aqkv, aqkv, gates, cw, gp, bd)


def _gdn_consts(group):
    c = GDN_CHUNK
    w = A_W
    r = group * c
    i = np.arange(c)[:, None]
    j = np.arange(w)[None, :] % c
    col_head = np.arange(w)[None, :] // c
    row = np.arange(w)[:, None]
    t = np.arange(r)
    same_chunk = (t[:, None] // c) == (t[None, :] // c)
    consts = {}
    for name, rev in (("f", False), ("r", True)):
        ge = (i <= j) if rev else (i >= j)
        consts["tril_" + name] = ge.astype(np.float32)
        consts["strict_" + name] = (ge & (i != j)).astype(np.float32)
        tri = (t[None, :] >= t[:, None]) if rev else (t[None, :] <= t[:, None])
        consts["cum_" + name] = (tri & same_chunk).astype(np.float32)
        upper = (i >= j) if rev else (i <= j)
        consts["upper_" + name] = np.tile(upper, (group, 1)).astype(np.float32)
    blk16 = (i // 16) == (j // 16)
    blk32 = (i // 32) == (j // 32)
    consts["m16"] = blk16.astype(np.float32)
    consts["m32"] = (blk32 & ~blk16).astype(np.float32)
    consts["m64"] = (~blk32).astype(np.float32)
    consts["eye"] = (i == j).astype(np.float32)
    consts["bd"] = ((row // c) == col_head).astype(np.float32)
    consts["bd16"] = consts["bd"]
    consts["tot"] = (np.arange(16)[:, None] == (t[None, :] // c)).astype(np.float32)
    for d, name in enumerate(("f", "r")):
        sel = np.zeros((LANES, 2 * w), np.float32)
        for part in range(2):
            for h in range(GDN_HEADS):
                lane = part * 2 * GDN_HEADS + d * GDN_HEADS + h
                sel[lane, part * w + h * c: part * w + (h + 1) * c] = 1.0
        consts["sel_" + name] = sel
    return consts


_GDN_CONST_ORDER = ("tril_f", "strict_f", "cum_f", "upper_f", "sel_f", "tril_r", "strict_r", "cum_r", "upper_r",
                    "sel_r", "m16", "m32", "m64", "eye", "bd", "bd16", "tot")
_GDN_BF16_CONSTS = ("cum_f", "cum_r", "sel_f", "sel_r", "bd16", "tot")


def _expand(x16, bd16):
    return jnp.concatenate([x16] * GDN_HEADS, axis=0) * bd16


def _gdn_chains_prep(chains, cst):
    c = GDN_CHUNK
    w = A_W
    bd16 = cst["bd16"]
    eye = cst["eye"]
    for ch in chains:
        kq = _dot_nt(jnp.concatenate([ch["kb"], ch["q"]], axis=0).astype(BF16), _expand(ch["k"].astype(BF16), bd16))
        ch["low"] = kq[0:c] * ch["decay"] * cst["strict_" + ch["name"]]
        ch["intra"] = (kq[c:2 * c] * ch["decay"] * cst["tril_" + ch["name"]]).astype(BF16)
    for ch in chains:
        n1 = -(ch["low"] * cst["m16"])
        ch["p"] = eye + n1
        ch["nb"] = n1.astype(BF16)
    for ch in chains:
        ch["nb"] = _dot(ch["nb"], _expand(ch["nb"], bd16)).astype(BF16)
    for _ in range(2):
        for ch in chains:
            r = _dot(jnp.concatenate([ch["nb"], ch["p"].astype(BF16)], axis=0), _expand(ch["nb"], bd16))
            ch["nb"] = r[0:c].astype(BF16)
            ch["p"] = ch["p"] + r[c:2 * c]
    for ch in chains:
        ch["inv"] = ch["p"] + _dot(ch["p"].astype(BF16), _expand(ch["nb"], bd16))
    for mname in ("m32", "m64"):
        for ch in chains:
            ch["invb"] = ch["inv"].astype(BF16)
            ch["t1"] = _dot((ch["low"] * cst[mname]).astype(BF16), _expand(ch["invb"], bd16)).astype(BF16)
        for ch in chains:
            ch["inv"] = ch["inv"] - _dot(ch["invb"], _expand(ch["t1"], bd16))
    for ch in chains:
        rhs = jnp.concatenate([_expand(ch["vb"].astype(BF16), bd16), _expand(ch["kbg"].astype(BF16), bd16)], axis=1)
        uk = _dot(ch["inv"].astype(BF16), rhs)
        ch["u"] = uk[:, 0:w]
        ch["kq_lhs"] = jnp.concatenate([uk[:, w:2 * w], ch["qg"]], axis=0).astype(BF16)


def _gdn_scan_step(states, chs, cst):
    c = GDN_CHUNK
    ks = [_dot(ch["kq_lhs"], st.astype(BF16)) for st, ch in zip(states, chs)]
    v16 = [(ch["u"] - k_[0:c]).astype(BF16) for k_, ch in zip(ks, chs)]
    upd = [_dot_tn(ch["kd"], v) for v, ch in zip(v16, chs)]
    outs = [k_[c:2 * c] + _dot(ch["intra"], _expand(v, cst["bd16"])) for k_, v, ch in zip(ks, v16, chs)]
    states = [st * ch["egl"] + u_ * cst["bd"] for st, u_, ch in zip(states, upd, chs)]
    return states, outs


def _gdn_block_gates(qkv_ref, gact_ref, cst, name, group):
    c = GDN_CHUNK
    w = A_W
    r = group * c
    qkv = qkv_ref[...]
    q = qkv[:, 0:w]
    k = qkv[:, w:2 * w]
    v = qkv[:, 2 * w:3 * w]
    gsel = _dot_exact_rhs(gact_ref[...], cst["sel_" + name])
    gb = gsel[:, 0:w]
    bb = gsel[:, w:2 * w]
    gc = _dot_exact_lhs(cst["cum_" + name], gb)
    gl_rows = _dot_exact_lhs(cst["tot"], gb)
    gr_rows = _dot_exact_lhs(cst["tot"], gb * cst["upper_" + name])
    kb = k * bb
    vb = v * bb
    eg = jnp.exp(gc)
    kbg = kb * eg
    qg = q * eg
    chunks = []
    for g in range(group):
        rows = slice(g * c, (g + 1) * c)
        gl = gl_rows[g:g + 1]
        chunks.append(dict(name=name, q=q[rows], k=k[rows], kb=kb[rows], vb=vb[rows], kbg=kbg[rows],
                           decay=jnp.exp(jnp.minimum(gc[rows] - gr_rows[g:g + 1], 0.0)), qg=qg[rows],
                           kd=(k[rows] * jnp.exp(gl - gc[rows])).astype(BF16), egl=jnp.exp(gl)))
    return chunks


def _gdn_scan_kernel(*refs, group):
    nconst = len(_GDN_CONST_ORDER)
    qkv_f_ref, gact_f_ref, qkv_r_ref, gact_r_ref = refs[0:4]
    cst = {n: r[...] for n, r in zip(_GDN_CONST_ORDER, refs[4:4 + nconst])}
    of_ref, or_ref, sf_ref, sr_ref = refs[4 + nconst:]

    @pl.when(pl.program_id(1) == 0)
    def _():
        sf_ref[...] = jnp.zeros_like(sf_ref)
        sr_ref[...] = jnp.zeros_like(sr_ref)

    c = GDN_CHUNK
    chunks_f = _gdn_block_gates(qkv_f_ref, gact_f_ref, cst, "f", group)
    chunks_r = _gdn_block_gates(qkv_r_ref, gact_r_ref, cst, "r", group)
    _gdn_chains_prep(chunks_f + chunks_r, cst)
    states = [sf_ref[...], sr_ref[...]]
    for step in range(group):
        gf = step
        gr = group - 1 - step
        states, (out_f, out_r) = _gdn_scan_step(states, [chunks_f[gf], chunks_r[gr]], cst)
        of_ref[gf * c:(gf + 1) * c, :] = out_f
        or_ref[gr * c:(gr + 1) * c, :] = out_r
    sf_ref[...] = states[0]
    sr_ref[...] = states[1]


def _gdn_scan(qkv, gact, group=GDN_GROUP):
    b, s, w3 = qkv.shape
    group = min(group, s // GDN_CHUNK)
    r = group * GDN_CHUNK
    nblk = s // r
    consts = _gdn_consts(group)
    const_arrays = [jnp.asarray(consts[n], BF16 if n in _GDN_BF16_CONSTS else F32) for n in _GDN_CONST_ORDER]
    fwd = lambda bi, ci: (bi, ci, 0)
    rev = lambda bi, ci: (bi, nblk - 1 - ci, 0)
    in_specs = [pl.BlockSpec((None, r, w3), fwd), pl.BlockSpec((None, r, LANES), fwd),
                pl.BlockSpec((None, r, w3), rev), pl.BlockSpec((None, r, LANES), rev)]
    in_specs += [pl.BlockSpec(a.shape, lambda bi, ci: (0, 0)) for a in const_arrays]
    return pl.pallas_call(
        functools.partial(_gdn_scan_kernel, group=group),
        grid=(b, nblk),
        in_specs=in_specs,
        out_specs=[pl.BlockSpec((None, r, A_W), fwd), pl.BlockSpec((None, r, A_W), rev)],
        out_shape=[jax.ShapeDtypeStruct((b, s, A_W), F32)] * 2,
        scratch_shapes=[pltpu.VMEM((A_W, A_W), F32)] * 2,
        compiler_params=_cparams("parallel", "arbitrary"),
        name="gdn_scan",
    )(qkv, gact, qkv, gact, *const_arrays)


def _dilated_bias(tq):
    r = DIL_REACH
    ii = np.arange(tq)[:, None]
    m = np.arange(tq + 4 * r)[None, :]
    d = 2 * r + ii - m
    count = np.zeros(d.shape, np.float64)
    for window, dil in DIL_PATTERNS:
        count += ((d % dil) == 0) & (np.abs(d) <= window // 2)
    bias = np.where(count > 0, np.log(np.maximum(count, 1.0)), NEG_BIG).astype(np.float32)
    return jnp.asarray(bias.reshape(tq, -1, tq).transpose(1, 0, 2))


def _dilated_kernel(q_ref, k_ref, v_ref, bias_ref, o_ref):
    tq = q_ref.shape[0]
    s = k_ref.shape[0]
    nwin = 1 + 2 * DIL_REACH // tq
    wl = nwin * tq
    q0 = pl.program_id(2) * tq
    ws = jnp.clip(q0 - DIL_REACH, 0, s - wl)
    mb0 = (2 * DIL_REACH - (q0 - ws)) // tq
    ws = pl.multiple_of(ws, tq)
    q = q_ref[...]
    kw = k_ref[pl.ds(ws, wl), :]
    vw = v_ref[pl.ds(ws, wl), :]
    bias = jnp.concatenate([bias_ref[mb0 + t] for t in range(nwin)], axis=1)
    lane = lax.broadcasted_iota(jnp.int32, q.shape, 1)
    outs = []
    for h in range(2):
        in_head = (lane // HEAD_DIM) == h
        sc = _dot_nt(jnp.where(in_head, q, jnp.zeros_like(q)), kw) + bias
        mx = jnp.max(sc, axis=-1, keepdims=True)
        e = jnp.exp(sc - mx)
        den = jnp.sum(e, axis=-1, keepdims=True)
        outs.append(_dot(e.astype(BF16), vw) / den)
    o_ref[...] = jnp.where(lane < HEAD_DIM, outs[0], outs[1]).astype(o_ref.dtype)


def _dilated(bqkv):
    b, s, _ = bqkv.shape
    tq = DIL_TQ
    npair = B_W // LANES
    bias = _dilated_bias(tq)
    return pl.pallas_call(
        _dilated_kernel,
        grid=(b, npair, s // tq),
        in_specs=[pl.BlockSpec((None, tq, LANES), lambda bi, p, i: (bi, i, p)),
                  pl.BlockSpec((None, s, LANES), lambda bi, p, i: (bi, 0, npair + p)),
                  pl.BlockSpec((None, s, LANES), lambda bi, p, i: (bi, 0, 2 * npair + p)),
                  pl.BlockSpec(bias.shape, lambda bi, p, i: (0, 0, 0))],
        out_specs=pl.BlockSpec((None, tq, LANES), lambda bi, p, i: (bi, i, p)),
        out_shape=jax.ShapeDtypeStruct((b, s, B_W), BF16),
        compiler_params=_cparams("parallel", "parallel", "arbitrary"),
        name="dilated_attn",
    )(bqkv, bqkv, bqkv, bias)


DIFF_MIN_DENOM = 2.0 ** -80


def _diff_kernel(lam_ref, q_ref, k_ref, v_ref, sw_ref, o_ref, kaug_ref, vaug_ref, knorm_ref, acc_ref,
                 *, lambda_init, kc, unroll):
    tq = q_ref.shape[0]
    s = k_ref.shape[0]
    nchunk = s // kc

    @pl.when(pl.program_id(2) == 0)
    def _():
        lane = lax.broadcasted_iota(jnp.int32, (s, LANES), 1)
        k = k_ref[...]
        kaug_ref[:, 0:LANES] = k
        kaug_ref[:, LANES:] = jnp.where(lane < 3, 1.0, 0.0).astype(BF16)
        vaug_ref[:, 0:LANES] = v_ref[...]
        vaug_ref[:, LANES:] = jnp.ones((s, LANES), BF16)
        kk = k.astype(F32)
        kk = kk * kk
        for comp in range(2):
            in_comp = (lane // HEAD_DIM) == comp
            n2 = jnp.sum(jnp.where(in_comp, kk, 0.0), axis=-1, keepdims=True)
            knorm_ref[comp:comp + 1, :] = jnp.broadcast_to(jnp.max(n2, axis=0, keepdims=True), (1, LANES))

    lp = lam_ref[...]
    lam = (jnp.exp(jnp.sum(lp[0:1] * lp[1:2], axis=-1, keepdims=True))
           - jnp.exp(jnp.sum(lp[2:3] * lp[3:4], axis=-1, keepdims=True)) + lambda_init)
    q = q_ref[...]
    lane = lax.broadcasted_iota(jnp.int32, q.shape, 1)
    zero = jnp.zeros_like(q)
    qs = jnp.concatenate([jnp.where(lane < HEAD_DIM, q, zero), jnp.where(lane >= HEAD_DIM, q, zero)], axis=0)

    def accumulate(shift):
        hi, mid, lo = (t.astype(F32) for t in _split3(-shift))
        lane2 = lax.broadcasted_iota(jnp.int32, (2 * tq, LANES), 1)
        m_cols = jnp.where(lane2 == 0, hi, jnp.where(lane2 == 1, mid, jnp.where(lane2 == 2, lo, 0.0)))
        q_aug = jnp.concatenate([qs, m_cols.astype(BF16)], axis=1)

        def acc_body(j, acc):
            start = pl.multiple_of(j * kc, kc)
            p = jnp.exp2(_dot_nt(q_aug, kaug_ref[pl.ds(start, kc), :])).astype(BF16)
            return acc + _dot(p, vaug_ref[pl.ds(start, kc), :])

        return lax.fori_loop(0, nchunk, acc_body, jnp.zeros((2 * tq, 2 * LANES), F32), unroll=unroll)

    qf = qs.astype(F32)
    q2 = jnp.sum(qf * qf, axis=-1, keepdims=True)
    row = lax.broadcasted_iota(jnp.int32, (2 * tq, 1), 0)
    k2 = jnp.where(row < tq, knorm_ref[0:1, 0:1], knorm_ref[1:2, 0:1])
    bound = jnp.sqrt(q2 * k2) * 1.001 + 1e-30
    acc = accumulate(bound)
    acc_ref[...] = acc
    denom_ok = jnp.min(acc[:, LANES:LANES + 1]) >= DIFF_MIN_DENOM

    @pl.when(jnp.logical_not(denom_ok))
    def _():
        def max_body(j, m_lane):
            start = pl.multiple_of(j * kc, kc)
            sc = _dot_nt(qs, k_ref[pl.ds(start, kc), :])
            for c0 in range(0, kc, LANES):
                m_lane = jnp.maximum(m_lane, sc[:, c0:c0 + LANES])
            return m_lane

        m_lane = lax.fori_loop(0, nchunk, max_body, jnp.full((2 * tq, LANES), NEG_BIG, F32))
        acc_ref[...] = accumulate(jnp.max(m_lane, axis=-1, keepdims=True))

    acc = acc_ref[...]
    o = acc[0:tq, 0:LANES] / acc[0:tq, LANES:] - lam * (acc[tq:, 0:LANES] / acc[tq:, LANES:])
    o = o * lax.rsqrt(jnp.mean(o * o, axis=-1, keepdims=True) + NORM_EPS) * sw_ref[...]
    o_ref[...] = (o * (1.0 - lambda_init)).astype(o_ref.dtype)


def _diff_attn(cq, ck, cv, lam_params, subln_w, lambda_init):
    b, s, _ = cq.shape
    tq = min(DIFF_TQ, s)
    kc = min(DIFF_KC, s)
    qspec = pl.BlockSpec((None, tq, LANES), lambda bi, h, i: (bi, i, h))
    kvspec = pl.BlockSpec((None, s, LANES), lambda bi, h, i: (bi, 0, h))
    return pl.pallas_call(
        functools.partial(_diff_kernel, lambda_init=lambda_init, kc=kc, unroll=min(DIFF_UNROLL, s // kc)),
        grid=(b, DIFF_HEADS, s // tq),
        in_specs=[pl.BlockSpec((8, LANES), lambda bi, h, i: (0, 0)), qspec, kvspec, kvspec,
                  pl.BlockSpec((1, LANES), lambda bi, h, i: (0, 0))],
        out_specs=qspec,
        out_shape=jax.ShapeDtypeStruct((b, s, C_W), BF16),
        scratch_shapes=[pltpu.VMEM((s, 2 * LANES), BF16)] * 2
        + [pltpu.VMEM((8, LANES), F32), pltpu.VMEM((2 * tq, 2 * LANES), F32)],
        compiler_params=_cparams("parallel", "parallel", "arbitrary"),
        name="diff_attn",
    )(lam_params, cq, ck, cv, subln_w.reshape(1, LANES).astype(F32))


def _outproj_kernel(x_ref, of_ref, or_ref, z_ref, ob_ref, oc_ref, gw_ref, bd_ref, w_ref, y_ref):
    o = of_ref[...] + or_ref[...]
    ms = _dot_exact_rhs(o * o, bd_ref[...]) * (1.0 / HEAD_DIM)
    oa = o * lax.rsqrt(ms + NORM_EPS) * gw_ref[...] * _silu(z_ref[...])
    acc = x_ref[...] + _dot(oa.astype(BF16), w_ref[0:A_W, :])
    acc += _dot(ob_ref[...], w_ref[A_W:A_W + B_W, :])
    acc += _dot(oc_ref[...], w_ref[A_W + B_W:, :])
    y_ref[...] = acc


def _outproj(x2d, o_f, o_r, z, o_b, o_c, gdn_norm_w, w_out, tm=512):
    t, d = x2d.shape
    tm = min(tm, t)
    rows = lambda width: pl.BlockSpec((tm, width), lambda i: (i, 0))
    const = lambda shape: pl.BlockSpec(shape, lambda i: (0, 0))
    head = np.arange(A_W) // HEAD_DIM
    bd = jnp.asarray(head[:, None] == head[None, :], BF16)
    gw = jnp.tile(gdn_norm_w.astype(F32), GDN_HEADS)[None, :]
    return pl.pallas_call(
        _outproj_kernel,
        grid=(t // tm,),
        in_specs=[rows(d), rows(A_W), rows(A_W), rows(A_W), rows(B_W), rows(C_W),
                  const((1, A_W)), const((A_W, A_W)), const(w_out.shape)],
        out_specs=rows(d),
        out_shape=jax.ShapeDtypeStruct((t, d), F32),
        compiler_params=_cparams("parallel"),
        name="outproj",
    )(x2d, o_f, o_r, z, o_b, o_c, gw, bd, w_out)


def _ffn_kernel(x_ref, nw_ref, wg_ref, wu_ref, wd_ref, fw_ref, y_ref, *, fc, final_norm):
    x = x_ref[...]
    h = (x * lax.rsqrt(jnp.mean(x * x, axis=-1, keepdims=True) + NORM_EPS) * nw_ref[...]).astype(BF16)
    acc = x
    for c0 in range(0, wg_ref.shape[1], fc):
        g = _dot(h, wg_ref[:, c0:c0 + fc])
        u = _dot(h, wu_ref[:, c0:c0 + fc])
        acc = acc + _dot((_silu(g) * u).astype(BF16), wd_ref[c0:c0 + fc, :])
    if final_norm:
        acc = acc * lax.rsqrt(jnp.mean(acc * acc, axis=-1, keepdims=True) + NORM_EPS) * fw_ref[...]
    y_ref[...] = acc


def _ffn(x2d, norm_w, wg, wu, wd, final_w, final_norm, tm=512, fc=256):
    t, d = x2d.shape
    tm = min(tm, t)
    f = wg.shape[1]
    rows = pl.BlockSpec((tm, d), lambda i: (i, 0))
    const = lambda shape: pl.BlockSpec(shape, lambda i: (0, 0), pipeline_mode=pl.Buffered(1))
    return pl.pallas_call(
        functools.partial(_ffn_kernel, fc=fc, final_norm=final_norm),
        grid=(t // tm,),
        in_specs=[rows, const((1, d)), const((d, f)), const((d, f)), const((f, d)), const((1, d))],
        out_specs=rows,
        out_shape=jax.ShapeDtypeStruct((t, d), F32),
        compiler_params=_cparams("parallel"),
        name="ffn",
    )(x2d, norm_w.reshape(1, d).astype(F32), wg, wu, wd, final_w.reshape(1, d).astype(F32))


def _pad_in_weight(w_in):
    gate_end = 4 * A_W + GATE_W
    d = w_in.shape[0]
    return jnp.concatenate([w_in[:, :gate_end], jnp.zeros((d, LANES - GATE_W), w_in.dtype),
                            w_in[:, gate_end:]], axis=1).astype(BF16)


def _layer(x2d, b, s, cos, sin, attn_norm_w, w_in, conv_w, a_log, dt_bias, gdn_norm_w,
           lq1, lk1, lq2, lk2, subln_w, w_out, ffn_norm_w, w_gate, w_up, w_down,
           final_w, lambda_init, last):
    aqkv, az, gates, bqkv, cq, ck, cv = _norm_inproj(x2d, attn_norm_w.astype(F32), _pad_in_weight(w_in), cos, sin)
    qkv, gact = _gdn_prep(aqkv.reshape(b, s, -1), gates.reshape(b, s, -1), conv_w, a_log, dt_bias)
    o_f, o_r = _gdn_scan(qkv, gact)
    o_b = _dilated(bqkv.reshape(b, s, -1))
    lam_params = jnp.zeros((8, LANES), F32).at[0:4, :HEAD_DIM].set(
        jnp.stack([lq1, lk1, lq2, lk2]).astype(F32))
    o_c = _diff_attn(cq.reshape(b, s, -1), ck.reshape(b, s, -1), cv.reshape(b, s, -1),
                     lam_params, subln_w, lambda_init)
    t = b * s
    x2d = _outproj(x2d, o_f.reshape(t, -1), o_r.reshape(t, -1), az, o_b.reshape(t, -1),
                   o_c.reshape(t, -1), gdn_norm_w, w_out.astype(BF16))
    return _ffn(x2d, ffn_norm_w, w_gate.astype(BF16), w_up.astype(BF16), w_down.astype(BF16),
                final_w, last)


def kernel(x, positions, attn_norm_w, w_in, conv_w, a_log, dt_bias, gdn_norm_w, lambda_q1, lambda_k1,
           lambda_q2, lambda_k2, subln_w, w_out, ffn_norm_w, w_gate, w_up, w_down, final_norm_w):
    b, s, d = x.shape
    depth = w_in.shape[0]
    cos, sin = _rope_tables(positions)
    x2d = x.reshape(b * s, d)
    for l in range(depth):
        lambda_init = 0.8 - 0.6 * math.exp(-0.3 * l)
        x2d = _layer(x2d, b, s, cos, sin, attn_norm_w[l], w_in[l], conv_w[l], a_log[l], dt_bias[l],
                     gdn_norm_w[l], lambda_q1[l], lambda_k1[l], lambda_q2[l], lambda_k2[l], subln_w[l],
                     w_out[l], ffn_norm_w[l], w_gate[l], w_up[l], w_down[l], final_norm_w,
                     lambda_init, l == depth - 1)
    return x2d.reshape(b, s, d)
```

```python
import functools
import math

import numpy as np
import jax
import jax.numpy as jnp
from jax import lax
from jax.experimental import pallas as pl
from jax.experimental.pallas import tpu as pltpu

F32 = jnp.float32
BF16 = jnp.bfloat16

NORM_EPS = 1e-6
ROPE_THETA = 10000.0
HEAD_DIM = 64
LANES = 128

GDN_HEADS = 4
GDN_CHUNK = 64
GDN_GROUP = 4
CONV_K = 5
CONV_HALO = 8

DIL_HEADS = 4
DIL_PATTERNS = ((128, 1), (512, 4), (2048, 16))
DIL_REACH = max(w // 2 for w, _ in DIL_PATTERNS)
DIL_TQ = 256
DIL_BPC = 1

DIFF_HEADS = 4
DIFF_TQ = 256
DIFF_KC = 512

A_W = GDN_HEADS * HEAD_DIM
B_W = DIL_HEADS * HEAD_DIM
C_W = DIFF_HEADS * 2 * HEAD_DIM
GATE_W = 2 * 2 * GDN_HEADS
NEG_BIG = -1e30
LOG2E = math.log2(math.e)

VMEM_LIMIT = 56 * 1024 * 1024


def _cparams(*sem):
    return pltpu.CompilerParams(dimension_semantics=sem, vmem_limit_bytes=VMEM_LIMIT)


def _split3(x):
    hi = x.astype(BF16)
    r1 = x - hi.astype(F32)
    mid = r1.astype(BF16)
    lo = (r1 - mid.astype(F32)).astype(BF16)
    return hi, mid, lo


def _dot(a, b):
    return jnp.dot(a, b, preferred_element_type=F32)


def _dot_nt(a, b):
    return lax.dot_general(a, b, (((1,), (1,)), ((), ())), preferred_element_type=F32)


def _dot_tn(a, b):
    return lax.dot_general(a, b, (((0,), (0,)), ((), ())), preferred_element_type=F32)


def _dot_exact_rhs(x, m_bf16):
    hi, mid, lo = _split3(x)
    return _dot(hi, m_bf16) + _dot(mid, m_bf16) + _dot(lo, m_bf16)


def _dot_exact_lhs(m_bf16, x):
    hi, mid, lo = _split3(x)
    return _dot(m_bf16, hi) + _dot(m_bf16, mid) + _dot(m_bf16, lo)


def _sigmoid(x):
    return 1.0 / (1.0 + jnp.exp(-x))


def _silu(x):
    return x * _sigmoid(x)


def _rope_kernel(pos_ref, inv_ref, sign_ref, cos_ref, sin_ref):
    ang = pos_ref[...].astype(F32) * inv_ref[...]
    cos_ref[...] = jnp.cos(ang)
    sin_ref[...] = jnp.sin(ang) * sign_ref[...]


def _rope_tables(positions):
    t = positions.size
    tr = min(t, 1024)
    half = HEAD_DIM // 2
    inv = ROPE_THETA ** (-jnp.arange(0, HEAD_DIM, 2, dtype=F32) / HEAD_DIM)
    inv_row = jnp.tile(inv, LANES // half)[None, :]
    sign_row = jnp.asarray(np.where((np.arange(LANES) % HEAD_DIM) < half, -1.0, 1.0), F32)[None, :]
    row = pl.BlockSpec((1, LANES), lambda i: (0, 0))
    out = pl.BlockSpec((tr, LANES), lambda i: (i, 0))
    return pl.pallas_call(
        _rope_kernel,
        grid=(t // tr,),
        in_specs=[pl.BlockSpec((tr, 1), lambda i: (i, 0)), row, row],
        out_specs=[out, out],
        out_shape=[jax.ShapeDtypeStruct((t, LANES), F32)] * 2,
        compiler_params=_cparams("parallel"),
        name="rope_tables",
    )(positions.reshape(t, 1), inv_row, sign_row)


def _rope(y, cos, sin):
    half = HEAD_DIM // 2
    lane = lax.broadcasted_iota(jnp.int32, cos.shape, 1)
    first_half = (lane % HEAD_DIM) < half
    slabs = []
    for c0 in range(0, y.shape[1], LANES):
        ys = y[:, c0:c0 + LANES]
        partner = jnp.where(first_half, pltpu.roll(ys, LANES - half, 1), pltpu.roll(ys, half, 1))
        slabs.append(ys * cos + partner * sin)
    return jnp.concatenate(slabs, axis=1)


def _inproj_kernel(x_ref, nw_ref, w_ref, cos_ref, sin_ref,
                   aqkv_ref, az_ref, gate_ref, bqkv_ref, cq_ref, ck_ref, cv_ref):
    x = x_ref[...]
    h = x * lax.rsqrt(jnp.mean(x * x, axis=-1, keepdims=True) + NORM_EPS) * nw_ref[...]
    h = h.astype(BF16)
    cos = cos_ref[...]
    sin = sin_ref[...]
    scale = HEAD_DIM ** -0.5 * LOG2E

    def proj(start, width):
        return _dot(h, w_ref[:, start:start + width])

    o = 0
    aqkv_ref[...] = proj(o, 3 * A_W)
    o += 3 * A_W
    az_ref[...] = proj(o, A_W)
    o += A_W
    gate_ref[...] = proj(o, LANES)
    o += LANES
    bqkv_ref[:, 0:B_W] = (_rope(proj(o, B_W), cos, sin) * scale).astype(BF16)
    o += B_W
    bqkv_ref[:, B_W:2 * B_W] = _rope(proj(o, B_W), cos, sin).astype(BF16)
    o += B_W
    bqkv_ref[:, 2 * B_W:3 * B_W] = proj(o, B_W).astype(BF16)
    o += B_W
    cq_ref[...] = (_rope(proj(o, C_W), cos, sin) * scale).astype(BF16)
    o += C_W
    ck_ref[...] = _rope(proj(o, C_W), cos, sin).astype(BF16)
    o += C_W
    cv_ref[...] = proj(o, C_W).astype(BF16)


def _norm_inproj(x2d, norm_w, w_pad, cos, sin, tm=512):
    t, d = x2d.shape
    tm = min(tm, t)
    np_ = w_pad.shape[1]
    rows = lambda width: pl.BlockSpec((tm, width), lambda i: (i, 0))
    out_w = (3 * A_W, A_W, LANES, 3 * B_W, C_W, C_W, C_W)
    out_dt = (F32, F32, F32, BF16, BF16, BF16, BF16)
    return pl.pallas_call(
        _inproj_kernel,
        grid=(t // tm,),
        in_specs=[rows(d), pl.BlockSpec((1, d), lambda i: (0, 0)),
                  pl.BlockSpec((d, np_), lambda i: (0, 0)), rows(LANES), rows(LANES)],
        out_specs=[rows(w) for w in out_w],
        out_shape=[jax.ShapeDtypeStruct((t, w), dt) for w, dt in zip(out_w, out_dt)],
        compiler_params=_cparams("parallel"),
        name="norm_inproj",
    )(x2d, norm_w.reshape(1, d), w_pad, cos, sin)


def _gdn_prep_kernel(prev_ref, cur_ref, next_ref, gate_ref, cw_ref, gp_ref, bd_ref,
                     qkv_ref, gact_ref, ext_ref):
    i = pl.program_id(1)
    n = pl.num_programs(1)
    tr = cur_ref.shape[0]
    ext_ref[0:CONV_HALO, :] = jnp.where(i > 0, prev_ref[...], 0.0)
    ext_ref[CONV_HALO:CONV_HALO + tr, :] = cur_ref[...]
    ext_ref[CONV_HALO + tr:, :] = jnp.where(i < n - 1, next_ref[...], 0.0)
    pad = (CONV_K - 1) // 2
    acc = None
    for j in range(CONV_K):
        term = ext_ref[pl.ds(CONV_HALO - pad + j, tr), :] * cw_ref[j:j + 1, :]
        acc = term if acc is None else acc + term
    y = _silu(acc)
    bd = bd_ref[...]
    dk_scale = HEAD_DIM ** -0.5
    for part, mul in ((0, dk_scale), (1, 1.0)):
        t = y[:, part * A_W:(part + 1) * A_W]
        ss = _dot_exact_rhs(t * t, bd)
        qkv_ref[:, part * A_W:(part + 1) * A_W] = t * (lax.rsqrt(ss + 1e-6) * mul)
    qkv_ref[:, 2 * A_W:] = y[:, 2 * A_W:]
    a = gate_ref[...]
    z = a + gp_ref[1:2, :]
    softplus = jnp.maximum(z, 0.0) + jnp.log(1.0 + jnp.exp(-jnp.abs(z)))
    g = gp_ref[0:1, :] * softplus
    lane = lax.broadcasted_iota(jnp.int32, a.shape, 1)
    gact_ref[...] = jnp.where(lane < GATE_W // 2, g, jnp.where(lane < GATE_W, _sigmoid(a), 0.0))


def _gdn_prep(aqkv, gates, conv_w, a_log, dt_bias, tr=512):
    b, s, w = aqkv.shape
    tr = min(tr, s)
    hb = tr // CONV_HALO
    nblk8 = s // CONV_HALO
    cw = jnp.zeros((8, w), F32).at[:CONV_K].set(conv_w.astype(F32))
    gp = jnp.zeros((8, LANES), F32)
    gp = gp.at[0, :GATE_W // 2].set(-jnp.exp(a_log.astype(F32).reshape(-1)))
    gp = gp.at[1, :GATE_W // 2].set(dt_bias.astype(F32).reshape(-1))
    head = np.arange(A_W) // HEAD_DIM
    bd = jnp.asarray(head[:, None] == head[None, :], BF16)
    const = lambda shape: pl.BlockSpec(shape, lambda bi, i: (0, 0))
    return pl.pallas_call(
        _gdn_prep_kernel,
        grid=(b, s // tr),
        in_specs=[
            pl.BlockSpec((None, CONV_HALO, w), lambda bi, i: (bi, jnp.maximum(i * hb - 1, 0), 0)),
            pl.BlockSpec((None, tr, w), lambda bi, i: (bi, i, 0)),
            pl.BlockSpec((None, CONV_HALO, w), lambda bi, i: (bi, jnp.minimum((i + 1) * hb, nblk8 - 1), 0)),
            pl.BlockSpec((None, tr, LANES), lambda bi, i: (bi, i, 0)),
            const((8, w)), const((8, LANES)), const((A_W, A_W)),
        ],
        out_specs=[pl.BlockSpec((None, tr, w), lambda bi, i: (bi, i, 0)),
                   pl.BlockSpec((None, tr, LANES), lambda bi, i: (bi, i, 0))],
        out_shape=[jax.ShapeDtypeStruct((b, s, w), F32), jax.ShapeDtypeStruct((b, s, LANES), F32)],
        scratch_shapes=[pltpu.VMEM((tr + 2 * CONV_HALO, w), F32)],
        compiler_params=_cparams("parallel", "parallel"),
        name="gdn_prep",
    )(aqkv, aqkv, aqkv, gates, cw, gp, bd)


def _gdn_consts(group):
    c = GDN_CHUNK
    w = A_W
    r = group * c
    i = np.arange(c)[:, None]
    j = np.arange(w)[None, :] % c
    col_head = np.arange(w)[None, :] // c
    row = np.arange(w)[:, None]
    t = np.arange(r)
    same_chunk = (t[:, None] // c) == (t[None, :] // c)
    consts = {}
    for name, rev in (("f", False), ("r", True)):
        ge = (i <= j) if rev else (i >= j)
        consts["tril_" + name] = ge.astype(np.float32)
        consts["strict_" + name] = (ge & (i != j)).astype(np.float32)
        tri = (t[None, :] >= t[:, None]) if rev else (t[None, :] <= t[:, None])
        consts["cum_" + name] = (tri & same_chunk).astype(np.float32)
        upper = (i >= j) if rev else (i <= j)
        consts["upper_" + name] = np.tile(upper, (group, 1)).astype(np.float32)
    blk16 = (i // 16) == (j // 16)
    blk32 = (i // 32) == (j // 32)
    consts["m16"] = blk16.astype(np.float32)
    consts["m32"] = (blk32 & ~blk16).astype(np.float32)
    consts["m64"] = (~blk32).astype(np.float32)
    consts["eye"] = (i == j).astype(np.float32)
    consts["bd"] = ((row // c) == col_head).astype(np.float32)
    consts["bd16"] = consts["bd"]
    consts["tot"] = (np.arange(16)[:, None] == (t[None, :] // c)).astype(np.float32)
    for d, name in enumerate(("f", "r")):
        sel = np.zeros((LANES, 2 * w), np.float32)
        for part in range(2):
            for h in range(GDN_HEADS):
                lane = part * 2 * GDN_HEADS + d * GDN_HEADS + h
                sel[lane, part * w + h * c: part * w + (h + 1) * c] = 1.0
        consts["sel_" + name] = sel
    return consts


_GDN_CONST_ORDER = ("tril_f", "strict_f", "cum_f", "upper_f", "sel_f", "tril_r", "strict_r", "cum_r", "upper_r",
                    "sel_r", "m16", "m32", "m64", "eye", "bd", "bd16", "tot")
_GDN_BF16_CONSTS = ("cum_f", "cum_r", "sel_f", "sel_r", "bd16", "tot")


def _expand(x16, bd16):
    return jnp.concatenate([x16] * GDN_HEADS, axis=0) * bd16


def _gdn_chains_prep(chains, cst):
    c = GDN_CHUNK
    w = A_W
    bd16 = cst["bd16"]
    eye = cst["eye"]
    for ch in chains:
        kq = _dot_nt(jnp.concatenate([ch["kb"], ch["q"]], axis=0).astype(BF16), _expand(ch["k"].astype(BF16), bd16))
        ch["low"] = kq[0:c] * ch["decay"] * cst["strict_" + ch["name"]]
        ch["intra"] = (kq[c:2 * c] * ch["decay"] * cst["tril_" + ch["name"]]).astype(BF16)
    for ch in chains:
        n1 = -(ch["low"] * cst["m16"])
        ch["p"] = eye + n1
        ch["nb"] = n1.astype(BF16)
    for ch in chains:
        ch["nb"] = _dot(ch["nb"], _expand(ch["nb"], bd16)).astype(BF16)
    for _ in range(2):
        for ch in chains:
            r = _dot(jnp.concatenate([ch["nb"], ch["p"].astype(BF16)], axis=0), _expand(ch["nb"], bd16))
            ch["nb"] = r[0:c].astype(BF16)
            ch["p"] = ch["p"] + r[c:2 * c]
    for ch in chains:
        ch["inv"] = ch["p"] + _dot(ch["p"].astype(BF16), _expand(ch["nb"], bd16))
    for mname in ("m32", "m64"):
        for ch in chains:
            ch["invb"] = ch["inv"].astype(BF16)
            ch["t1"] = _dot((ch["low"] * cst[mname]).astype(BF16), _expand(ch["invb"], bd16)).astype(BF16)
        for ch in chains:
            ch["inv"] = ch["inv"] - _dot(ch["invb"], _expand(ch["t1"], bd16))
    for ch in chains:
        rhs = jnp.concatenate([_expand(ch["vb"].astype(BF16), bd16), _expand(ch["kbg"].astype(BF16), bd16)], axis=1)
        uk = _dot(ch["inv"].astype(BF16), rhs)
        ch["u"] = uk[:, 0:w]
        ch["kq_lhs"] = jnp.concatenate([uk[:, w:2 * w], ch["qg"]], axis=0).astype(BF16)


def _gdn_scan_step(states, chs, cst):
    c = GDN_CHUNK
    ks = [_dot(ch["kq_lhs"], st.astype(BF16)) for st, ch in zip(states, chs)]
    v16 = [(ch["u"] - k_[0:c]).astype(BF16) for k_, ch in zip(ks, chs)]
    upd = [_dot_tn(ch["kd"], v) for v, ch in zip(v16, chs)]
    outs = [k_[c:2 * c] + _dot(ch["intra"], _expand(v, cst["bd16"])) for k_, v, ch in zip(ks, v16, chs)]
    states = [st * ch["egl"] + u_ * cst["bd"] for st, u_, ch in zip(states, upd, chs)]
    return states, outs


def _gdn_block_gates(qkv_ref, gact_ref, cst, name, group):
    c = GDN_CHUNK
    w = A_W
    r = group * c
    qkv = qkv_ref[...]
    q = qkv[:, 0:w]
    k = qkv[:, w:2 * w]
    v = qkv[:, 2 * w:3 * w]
    gsel = _dot_exact_rhs(gact_ref[...], cst["sel_" + name])
    gb = gsel[:, 0:w]
    bb = gsel[:, w:2 * w]
    gc = _dot_exact_lhs(cst["cum_" + name], gb)
    gl_rows = _dot_exact_lhs(cst["tot"], gb)
    gr_rows = _dot_exact_lhs(cst["tot"], gb * cst["upper_" + name])
    kb = k * bb
    vb = v * bb
    eg = jnp.exp(gc)
    kbg = kb * eg
    qg = q * eg
    chunks = []
    for g in range(group):
        rows = slice(g * c, (g + 1) * c)
        gl = gl_rows[g:g + 1]
        chunks.append(dict(name=name, q=q[rows], k=k[rows], kb=kb[rows], vb=vb[rows], kbg=kbg[rows],
                           decay=jnp.exp(jnp.minimum(gc[rows] - gr_rows[g:g + 1], 0.0)), qg=qg[rows],
                           kd=(k[rows] * jnp.exp(gl - gc[rows])).astype(BF16), egl=jnp.exp(gl)))
    return chunks


def _gdn_scan_kernel(*refs, group):
    nconst = len(_GDN_CONST_ORDER)
    qkv_f_ref, gact_f_ref, qkv_r_ref, gact_r_ref = refs[0:4]
    cst = {n: r[...] for n, r in zip(_GDN_CONST_ORDER, refs[4:4 + nconst])}
    of_ref, or_ref, sf_ref, sr_ref = refs[4 + nconst:]

    @pl.when(pl.program_id(1) == 0)
    def _():
        sf_ref[...] = jnp.zeros_like(sf_ref)
        sr_ref[...] = jnp.zeros_like(sr_ref)

    c = GDN_CHUNK
    chunks_f = _gdn_block_gates(qkv_f_ref, gact_f_ref, cst, "f", group)
    chunks_r = _gdn_block_gates(qkv_r_ref, gact_r_ref, cst, "r", group)
    _gdn_chains_prep(chunks_f + chunks_r, cst)
    states = [sf_ref[...], sr_ref[...]]
    for step in range(group):
        gf = step
        gr = group - 1 - step
        states, (out_f, out_r) = _gdn_scan_step(states, [chunks_f[gf], chunks_r[gr]], cst)
        of_ref[gf * c:(gf + 1) * c, :] = out_f
        or_ref[gr * c:(gr + 1) * c, :] = out_r
    sf_ref[...] = states[0]
    sr_ref[...] = states[1]


def _gdn_scan(qkv, gact, group=GDN_GROUP):
    b, s, w3 = qkv.shape
    group = min(group, s // GDN_CHUNK)
    r = group * GDN_CHUNK
    nblk = s // r
    consts = _gdn_consts(group)
    const_arrays = [jnp.asarray(consts[n], BF16 if n in _GDN_BF16_CONSTS else F32) for n in _GDN_CONST_ORDER]
    fwd = lambda bi, ci: (bi, ci, 0)
    rev = lambda bi, ci: (bi, nblk - 1 - ci, 0)
    in_specs = [pl.BlockSpec((None, r, w3), fwd), pl.BlockSpec((None, r, LANES), fwd),
                pl.BlockSpec((None, r, w3), rev), pl.BlockSpec((None, r, LANES), rev)]
    in_specs += [pl.BlockSpec(a.shape, lambda bi, ci: (0, 0)) for a in const_arrays]
    return pl.pallas_call(
        functools.partial(_gdn_scan_kernel, group=group),
        grid=(b, nblk),
        in_specs=in_specs,
        out_specs=[pl.BlockSpec((None, r, A_W), fwd), pl.BlockSpec((None, r, A_W), rev)],
        out_shape=[jax.ShapeDtypeStruct((b, s, A_W), F32)] * 2,
        scratch_shapes=[pltpu.VMEM((A_W, A_W), F32)] * 2,
        compiler_params=_cparams("parallel", "arbitrary"),
        name="gdn_scan",
    )(qkv, gact, qkv, gact, *const_arrays)


MIN_DENOM = 2.0 ** -80
ATTN_BLOCKS_PER_ITER = 2
SHIFT_LANES = 3


def _augment_kv(k_ref, v_ref, kaug_ref, vaug_ref):
    s = k_ref.shape[0]
    lane = lax.broadcasted_iota(jnp.int32, (s, LANES), 1)
    k = k_ref[...]
    kaug_ref[:, 0:LANES] = k
    kaug_ref[:, LANES:] = jnp.where(lane < SHIFT_LANES, 1.0, 0.0).astype(BF16)
    vaug_ref[:, 0:LANES] = v_ref[...]
    vaug_ref[:, LANES:] = jnp.ones((s, LANES), BF16)
    kk = k.astype(F32)
    kk = kk * kk
    norms = []
    for half in range(2):
        n2 = jnp.sum(jnp.where((lane // HEAD_DIM) == half, kk, 0.0), axis=-1, keepdims=True)
        norms.append(jnp.max(n2, axis=0, keepdims=True))
    return norms


def _stack_halves(q):
    lane = lax.broadcasted_iota(jnp.int32, q.shape, 1)
    zero = jnp.zeros_like(q)
    return jnp.concatenate([jnp.where(lane < HEAD_DIM, q, zero), jnp.where(lane >= HEAD_DIM, q, zero)], axis=0)


def _score_bound(qs, knorms):
    rows = qs.shape[0]
    qf = qs.astype(F32)
    q2 = jnp.sum(qf * qf, axis=-1, keepdims=True)
    row = lax.broadcasted_iota(jnp.int32, (rows, 1), 0)
    k2 = jnp.where(row < rows // 2, knorms[0], knorms[1])
    return jnp.sqrt(q2 * k2) * 1.001 + 1e-30


def _augment_q(qs, shift):
    rows = qs.shape[0]
    hi, mid, lo = (t.astype(F32) for t in _split3(-shift))
    lane = lax.broadcasted_iota(jnp.int32, (rows, LANES), 1)
    m_cols = jnp.where(lane == 0, hi, jnp.where(lane == 1, mid, jnp.where(lane == 2, lo, 0.0)))
    return jnp.concatenate([qs, m_cols.astype(BF16)], axis=1)


def _softmax_pv(q_aug, kaug_ref, vaug_ref, key0, nchunk, kc, align, unroll, weight_fn=None, clamp=False):
    rows = q_aug.shape[0]

    def scores(j):
        start = pl.multiple_of(key0 + j * kc, align)
        return _dot_nt(q_aug, kaug_ref[pl.ds(start, kc), :])

    def weighted_values(j, sc):
        start = pl.multiple_of(key0 + j * kc, align)
        p = jnp.exp2(jnp.minimum(sc, 0.0) if clamp else sc)
        if weight_fn is not None:
            p = p * weight_fn(j)
        return _dot(p.astype(BF16), vaug_ref[pl.ds(start, kc), :])

    if unroll is None:
        acc = None
        sc = scores(0)
        for j in range(nchunk):
            sc_next = scores(j + 1) if j + 1 < nchunk else None
            part = weighted_values(j, sc)
            acc = part if acc is None else acc + part
            sc = sc_next
        return acc

    def body(j, acc):
        return acc + weighted_values(j, scores(j))

    return lax.fori_loop(0, nchunk, body, jnp.zeros((rows, 2 * LANES), F32), unroll=unroll)


def _row_max(qs, k_ref, key0, nchunk, kc, align, weight_fn=None):
    def body(j, m_lane):
        start = pl.multiple_of(key0 + j * kc, align)
        sc = _dot_nt(qs, k_ref[pl.ds(start, kc), :])
        if weight_fn is not None:
            sc = jnp.where(weight_fn(j) > 0.0, sc, NEG_BIG)
        for c0 in range(0, kc, LANES):
            m_lane = jnp.maximum(m_lane, sc[:, c0:c0 + LANES])
        return m_lane

    m_lane = lax.fori_loop(0, nchunk, body, jnp.full((qs.shape[0], LANES), NEG_BIG, F32))
    return jnp.max(m_lane, axis=-1, keepdims=True)


def _attention_blocks(nblk, block_fn, finish_fn, knorms, k_ref, kaug_ref, vaug_ref):
    per_iter = ATTN_BLOCKS_PER_ITER if nblk % ATTN_BLOCKS_PER_ITER == 0 else 1

    def fast(ii, lmin):
        blocks = []
        for t in range(per_iter):
            i = ii * per_iter + t
            qs, key0, nchunk, kc, align, weight_fn = block_fn(i)
            blocks.append((i, _augment_q(qs, _score_bound(qs, knorms)), key0, nchunk, kc, align, weight_fn))
        for i, q_aug, key0, nchunk, kc, align, weight_fn in blocks:
            acc = _softmax_pv(q_aug, kaug_ref, vaug_ref, key0, nchunk, kc, align, None, weight_fn)
            finish_fn(i, acc)
            lmin = jnp.minimum(lmin, jnp.min(acc[:, LANES:].reshape(-1, 8, LANES), axis=0))
        return lmin

    lmin = lax.fori_loop(0, nblk // per_iter, fast, jnp.full((8, LANES), 3e38, F32))
    all_ok = jnp.min(lmin) >= MIN_DENOM

    @pl.when(jnp.logical_not(all_ok))
    def _():
        def exact(i, carry):
            qs, key0, nchunk, kc, align, weight_fn = block_fn(i)
            m = _row_max(qs, k_ref, key0, nchunk, kc, align, weight_fn)
            finish_fn(i, _softmax_pv(_augment_q(qs, m), kaug_ref, vaug_ref, key0, nchunk, kc, align, 1, weight_fn,
                                     clamp=weight_fn is not None))
            return carry

        lax.fori_loop(0, nblk, exact, 0)


def _dilated_counts(tq):
    r = DIL_REACH
    ii = np.arange(tq)[:, None]
    m = np.arange(tq + 4 * r)[None, :]
    d = 2 * r + ii - m
    count = np.zeros(d.shape, np.float32)
    for window, dil in DIL_PATTERNS:
        count += ((d % dil) == 0) & (np.abs(d) <= window // 2)
    return jnp.asarray(count.reshape(tq, -1, tq).transpose(1, 0, 2))


def _dilated_kernel(q_ref, k_ref, v_ref, cnt_ref, o_ref, kaug_ref, vaug_ref, *, tq, bpc):
    s = k_ref.shape[0]
    nwin = 1 + 2 * DIL_REACH // tq
    knorms = _augment_kv(k_ref, v_ref, kaug_ref, vaug_ref)

    def block(i):
        q0 = pl.multiple_of(i * tq, tq)
        ws = jnp.clip(q0 - DIL_REACH, 0, s - nwin * tq)
        mb0 = (2 * DIL_REACH - (q0 - ws)) // tq

        def weights(j):
            cnt = jnp.concatenate([cnt_ref[mb0 + j * bpc + t] for t in range(bpc)], axis=1)
            return jnp.concatenate([cnt, cnt], axis=0)

        qs = _stack_halves(q_ref[pl.ds(q0, tq), :])
        return qs, ws, nwin // bpc, bpc * tq, tq, weights

    def finish(i, acc):
        lane = lax.broadcasted_iota(jnp.int32, (tq, LANES), 1)
        out = jnp.where(lane < HEAD_DIM, acc[0:tq, 0:LANES] / acc[0:tq, LANES:],
                        acc[tq:, 0:LANES] / acc[tq:, LANES:])
        o_ref[pl.ds(pl.multiple_of(i * tq, tq), tq), :] = out.astype(o_ref.dtype)

    _attention_blocks(s // tq, block, finish, knorms, k_ref, kaug_ref, vaug_ref)


def _dilated(bqkv):
    b, s, _ = bqkv.shape
    tq = DIL_TQ
    npair = B_W // LANES
    counts = _dilated_counts(tq)
    nwin = 1 + 2 * DIL_REACH // tq
    slab = lambda off: pl.BlockSpec((None, s, LANES), lambda bi, p: (bi, 0, off + p))
    return pl.pallas_call(
        functools.partial(_dilated_kernel, tq=tq, bpc=DIL_BPC if nwin % DIL_BPC == 0 else 1),
        grid=(b, npair),
        in_specs=[slab(0), slab(npair), slab(2 * npair), pl.BlockSpec(counts.shape, lambda bi, p: (0, 0, 0))],
        out_specs=slab(0),
        out_shape=jax.ShapeDtypeStruct((b, s, B_W), BF16),
        scratch_shapes=[pltpu.VMEM((s, 2 * LANES), BF16)] * 2,
        compiler_params=_cparams("parallel", "parallel"),
        name="dilated_attn",
    )(bqkv, bqkv, bqkv, counts)


def _diff_kernel(lam_ref, q_ref, k_ref, v_ref, sw_ref, o_ref, kaug_ref, vaug_ref, *, lambda_init, tq, kc):
    s = k_ref.shape[0]
    knorms = _augment_kv(k_ref, v_ref, kaug_ref, vaug_ref)
    lp = lam_ref[...]
    lam = (jnp.exp(jnp.sum(lp[0:1] * lp[1:2], axis=-1, keepdims=True))
           - jnp.exp(jnp.sum(lp[2:3] * lp[3:4], axis=-1, keepdims=True)) + lambda_init)
    sw = sw_ref[...]

    def block(i):
        qs = _stack_halves(q_ref[pl.ds(pl.multiple_of(i * tq, tq), tq), :])
        return qs, 0, s // kc, kc, kc, None

    def finish(i, acc):
        o = acc[0:tq, 0:LANES] / acc[0:tq, LANES:] - lam * (acc[tq:, 0:LANES] / acc[tq:, LANES:])
        o = o * lax.rsqrt(jnp.mean(o * o, axis=-1, keepdims=True) + NORM_EPS) * sw
        o_ref[pl.ds(pl.multiple_of(i * tq, tq), tq), :] = (o * (1.0 - lambda_init)).astype(o_ref.dtype)

    _attention_blocks(s // tq, block, finish, knorms, k_ref, kaug_ref, vaug_ref)


def _diff_attn(cq, ck, cv, lam_params, subln_w, lambda_init):
    b, s, _ = cq.shape
    tq = min(DIFF_TQ, s)
    kc = min(DIFF_KC, s)
    slab = pl.BlockSpec((None, s, LANES), lambda bi, h: (bi, 0, h))
    return pl.pallas_call(
        functools.partial(_diff_kernel, lambda_init=lambda_init, tq=tq, kc=kc),
        grid=(b, DIFF_HEADS),
        in_specs=[pl.BlockSpec((8, LANES), lambda bi, h: (0, 0)), slab, slab, slab,
                  pl.BlockSpec((1, LANES), lambda bi, h: (0, 0))],
        out_specs=slab,
        out_shape=jax.ShapeDtypeStruct((b, s, C_W), BF16),
        scratch_shapes=[pltpu.VMEM((s, 2 * LANES), BF16)] * 2,
        compiler_params=_cparams("parallel", "parallel"),
        name="diff_attn",
    )(lam_params, cq, ck, cv, subln_w.reshape(1, LANES).astype(F32))


def _outproj_kernel(x_ref, of_ref, or_ref, z_ref, ob_ref, oc_ref, gw_ref, bd_ref, w_ref, y_ref):
    o = of_ref[...] + or_ref[...]
    ms = _dot_exact_rhs(o * o, bd_ref[...]) * (1.0 / HEAD_DIM)
    oa = o * lax.rsqrt(ms + NORM_EPS) * gw_ref[...] * _silu(z_ref[...])
    acc = x_ref[...] + _dot(oa.astype(BF16), w_ref[0:A_W, :])
    acc += _dot(ob_ref[...], w_ref[A_W:A_W + B_W, :])
    acc += _dot(oc_ref[...], w_ref[A_W + B_W:, :])
    y_ref[...] = acc


def _outproj(x2d, o_f, o_r, z, o_b, o_c, gdn_norm_w, w_out, tm=512):
    t, d = x2d.shape
    tm = min(tm, t)
    rows = lambda width: pl.BlockSpec((tm, width), lambda i: (i, 0))
    const = lambda shape: pl.BlockSpec(shape, lambda i: (0, 0))
    head = np.arange(A_W) // HEAD_DIM
    bd = jnp.asarray(head[:, None] == head[None, :], BF16)
    gw = jnp.tile(gdn_norm_w.astype(F32), GDN_HEADS)[None, :]
    return pl.pallas_call(
        _outproj_kernel,
        grid=(t // tm,),
        in_specs=[rows(d), rows(A_W), rows(A_W), rows(A_W), rows(B_W), rows(C_W),
                  const((1, A_W)), const((A_W, A_W)), const(w_out.shape)],
        out_specs=rows(d),
        out_shape=jax.ShapeDtypeStruct((t, d), F32),
        compiler_params=_cparams("parallel"),
        name="outproj",
    )(x2d, o_f, o_r, z, o_b, o_c, gw, bd, w_out)


def _ffn_kernel(x_ref, nw_ref, wg_ref, wu_ref, wd_ref, fw_ref, y_ref, *, fc, final_norm):
    x = x_ref[...]
    h = (x * lax.rsqrt(jnp.mean(x * x, axis=-1, keepdims=True) + NORM_EPS) * nw_ref[...]).astype(BF16)
    acc = x
    for c0 in range(0, wg_ref.shape[1], fc):
        g = _dot(h, wg_ref[:, c0:c0 + fc])
        u = _dot(h, wu_ref[:, c0:c0 + fc])
        acc = acc + _dot((_silu(g) * u).astype(BF16), wd_ref[c0:c0 + fc, :])
    if final_norm:
        acc = acc * lax.rsqrt(jnp.mean(acc * acc, axis=-1, keepdims=True) + NORM_EPS) * fw_ref[...]
    y_ref[...] = acc


def _ffn(x2d, norm_w, wg, wu, wd, final_w, final_norm, tm=512, fc=256):
    t, d = x2d.shape
    tm = min(tm, t)
    f = wg.shape[1]
    rows = pl.BlockSpec((tm, d), lambda i: (i, 0))
    const = lambda shape: pl.BlockSpec(shape, lambda i: (0, 0), pipeline_mode=pl.Buffered(1))
    return pl.pallas_call(
        functools.partial(_ffn_kernel, fc=fc, final_norm=final_norm),
        grid=(t // tm,),
        in_specs=[rows, const((1, d)), const((d, f)), const((d, f)), const((f, d)), const((1, d))],
        out_specs=rows,
        out_shape=jax.ShapeDtypeStruct((t, d), F32),
        compiler_params=_cparams("parallel"),
        name="ffn",
    )(x2d, norm_w.reshape(1, d).astype(F32), wg, wu, wd, final_w.reshape(1, d).astype(F32))


def _pad_in_weight(w_in):
    gate_end = 4 * A_W + GATE_W
    d = w_in.shape[0]
    return jnp.concatenate([w_in[:, :gate_end], jnp.zeros((d, LANES - GATE_W), w_in.dtype),
                            w_in[:, gate_end:]], axis=1).astype(BF16)


def _layer(x2d, b, s, cos, sin, attn_norm_w, w_in, conv_w, a_log, dt_bias, gdn_norm_w,
           lq1, lk1, lq2, lk2, subln_w, w_out, ffn_norm_w, w_gate, w_up, w_down,
           final_w, lambda_init, last):
    aqkv, az, gates, bqkv, cq, ck, cv = _norm_inproj(x2d, attn_norm_w.astype(F32), _pad_in_weight(w_in), cos, sin)
    qkv, gact = _gdn_prep(aqkv.reshape(b, s, -1), gates.reshape(b, s, -1), conv_w, a_log, dt_bias)
    o_f, o_r = _gdn_scan(qkv, gact)
    o_b = _dilated(bqkv.reshape(b, s, -1))
    lam_params = jnp.zeros((8, LANES), F32).at[0:4, :HEAD_DIM].set(
        jnp.stack([lq1, lk1, lq2, lk2]).astype(F32))
    o_c = _diff_attn(cq.reshape(b, s, -1), ck.reshape(b, s, -1), cv.reshape(b, s, -1),
                     lam_params, subln_w, lambda_init)
    t = b * s
    x2d = _outproj(x2d, o_f.reshape(t, -1), o_r.reshape(t, -1), az, o_b.reshape(t, -1),
                   o_c.reshape(t, -1), gdn_norm_w, w_out.astype(BF16))
    return _ffn(x2d, ffn_norm_w, w_gate.astype(BF16), w_up.astype(BF16), w_down.astype(BF16),
                final_w, last)


def kernel(x, positions, attn_norm_w, w_in, conv_w, a_log, dt_bias, gdn_norm_w, lambda_q1, lambda_k1,
           lambda_q2, lambda_k2, subln_w, w_out, ffn_norm_w, w_gate, w_up, w_down, final_norm_w):
    b, s, d = x.shape
    depth = w_in.shape[0]
    cos, sin = _rope_tables(positions)
    x2d = x.reshape(b * s, d)
    for l in range(depth):
        lambda_init = 0.8 - 0.6 * math.exp(-0.3 * l)
        x2d = _layer(x2d, b, s, cos, sin, attn_norm_w[l], w_in[l], conv_w[l], a_log[l], dt_bias[l],
                     gdn_norm_w[l], lambda_q1[l], lambda_k1[l], lambda_q2[l], lambda_k2[l], subln_w[l],
                     w_out[l], ffn_norm_w[l], w_gate[l], w_up[l], w_down[l], final_norm_w,
                     lambda_init, l == depth - 1)
    return x2d.reshape(b, s, d)
```

```python
import functools
import math

import numpy as np
import jax
import jax.numpy as jnp
from jax import lax
from jax.experimental import pallas as pl
from jax.experimental.pallas import tpu as pltpu

F32 = jnp.float32
BF16 = jnp.bfloat16

NORM_EPS = 1e-6
ROPE_THETA = 10000.0
HEAD_DIM = 64
LANES = 128

GDN_HEADS = 4
GDN_CHUNK = 64
GDN_GROUP = 4
CONV_K = 5
CONV_HALO = 8

DIL_HEADS = 4
DIL_PATTERNS = ((128, 1), (512, 4), (2048, 16))
DIL_REACH = max(w // 2 for w, _ in DIL_PATTERNS)
DIL_TQ = 256

DIFF_HEADS = 4
DIFF_TQ = 256
DIFF_KC = 512

A_W = GDN_HEADS * HEAD_DIM
B_W = DIL_HEADS * HEAD_DIM
C_W = DIFF_HEADS * 2 * HEAD_DIM
GATE_W = 2 * 2 * GDN_HEADS
NEG_BIG = -1e30
LOG2E = math.log2(math.e)

VMEM_LIMIT = 56 * 1024 * 1024


def _cparams(*sem):
    return pltpu.CompilerParams(dimension_semantics=sem, vmem_limit_bytes=VMEM_LIMIT)


def _split3(x):
    hi = x.astype(BF16)
    r1 = x - hi.astype(F32)
    mid = r1.astype(BF16)
    lo = (r1 - mid.astype(F32)).astype(BF16)
    return hi, mid, lo


def _dot(a, b):
    return jnp.dot(a, b, preferred_element_type=F32)


def _dot_nt(a, b):
    return lax.dot_general(a, b, (((1,), (1,)), ((), ())), preferred_element_type=F32)


def _dot_tn(a, b):
    return lax.dot_general(a, b, (((0,), (0,)), ((), ())), preferred_element_type=F32)


def _dot_exact_rhs(x, m_bf16):
    hi, mid, lo = _split3(x)
    return _dot(hi, m_bf16) + _dot(mid, m_bf16) + _dot(lo, m_bf16)


def _dot_exact_lhs(m_bf16, x):
    hi, mid, lo = _split3(x)
    return _dot(m_bf16, hi) + _dot(m_bf16, mid) + _dot(m_bf16, lo)


def _sigmoid(x):
    return 1.0 / (1.0 + jnp.exp(-x))


def _silu(x):
    return x * _sigmoid(x)


def _rope_kernel(pos_ref, inv_ref, sign_ref, cos_ref, sin_ref):
    ang = pos_ref[...].astype(F32) * inv_ref[...]
    cos_ref[...] = jnp.cos(ang)
    sin_ref[...] = jnp.sin(ang) * sign_ref[...]


def _rope_tables(positions):
    t = positions.size
    tr = min(t, 1024)
    half = HEAD_DIM // 2
    inv = ROPE_THETA ** (-jnp.arange(0, HEAD_DIM, 2, dtype=F32) / HEAD_DIM)
    inv_row = jnp.tile(inv, LANES // half)[None, :]
    sign_row = jnp.asarray(np.where((np.arange(LANES) % HEAD_DIM) < half, -1.0, 1.0), F32)[None, :]
    row = pl.BlockSpec((1, LANES), lambda i: (0, 0))
    out = pl.BlockSpec((tr, LANES), lambda i: (i, 0))
    return pl.pallas_call(
        _rope_kernel,
        grid=(t // tr,),
        in_specs=[pl.BlockSpec((tr, 1), lambda i: (i, 0)), row, row],
        out_specs=[out, out],
        out_shape=[jax.ShapeDtypeStruct((t, LANES), F32)] * 2,
        compiler_params=_cparams("parallel"),
        name="rope_tables",
    )(positions.reshape(t, 1), inv_row, sign_row)


def _rope(y, cos, sin):
    half = HEAD_DIM // 2
    lane = lax.broadcasted_iota(jnp.int32, cos.shape, 1)
    first_half = (lane % HEAD_DIM) < half
    slabs = []
    for c0 in range(0, y.shape[1], LANES):
        ys = y[:, c0:c0 + LANES]
        partner = jnp.where(first_half, pltpu.roll(ys, LANES - half, 1), pltpu.roll(ys, half, 1))
        slabs.append(ys * cos + partner * sin)
    return jnp.concatenate(slabs, axis=1)


def _inproj_kernel(x_ref, nw_ref, w_ref, cos_ref, sin_ref,
                   aqkv_ref, az_ref, gate_ref, bqkv_ref, cq_ref, ck_ref, cv_ref):
    x = x_ref[...]
    h = x * lax.rsqrt(jnp.mean(x * x, axis=-1, keepdims=True) + NORM_EPS) * nw_ref[...]
    h = h.astype(BF16)
    cos = cos_ref[...]
    sin = sin_ref[...]
    scale = HEAD_DIM ** -0.5 * LOG2E

    def proj(start, width):
        return _dot(h, w_ref[:, start:start + width])

    o = 0
    aqkv_ref[...] = proj(o, 3 * A_W)
    o += 3 * A_W
    az_ref[...] = proj(o, A_W)
    o += A_W
    gate_ref[...] = proj(o, LANES)
    o += LANES
    bqkv_ref[:, 0:B_W] = (_rope(proj(o, B_W), cos, sin) * scale).astype(BF16)
    o += B_W
    bqkv_ref[:, B_W:2 * B_W] = _rope(proj(o, B_W), cos, sin).astype(BF16)
    o += B_W
    bqkv_ref[:, 2 * B_W:3 * B_W] = proj(o, B_W).astype(BF16)
    o += B_W
    cq_ref[...] = (_rope(proj(o, C_W), cos, sin) * scale).astype(BF16)
    o += C_W
    ck_ref[...] = _rope(proj(o, C_W), cos, sin).astype(BF16)
    o += C_W
    cv_ref[...] = proj(o, C_W).astype(BF16)


def _norm_inproj(x2d, norm_w, w_pad, cos, sin, tm=512):
    t, d = x2d.shape
    tm = min(tm, t)
    np_ = w_pad.shape[1]
    rows = lambda width: pl.BlockSpec((tm, width), lambda i: (i, 0))
    out_w = (3 * A_W, A_W, LANES, 3 * B_W, C_W, C_W, C_W)
    out_dt = (F32, F32, F32, BF16, BF16, BF16, BF16)
    return pl.pallas_call(
        _inproj_kernel,
        grid=(t // tm,),
        in_specs=[rows(d), pl.BlockSpec((1, d), lambda i: (0, 0)),
                  pl.BlockSpec((d, np_), lambda i: (0, 0)), rows(LANES), rows(LANES)],
        out_specs=[rows(w) for w in out_w],
        out_shape=[jax.ShapeDtypeStruct((t, w), dt) for w, dt in zip(out_w, out_dt)],
        compiler_params=_cparams("parallel"),
        name="norm_inproj",
    )(x2d, norm_w.reshape(1, d), w_pad, cos, sin)


def _gdn_prep_kernel(prev_ref, cur_ref, next_ref, gate_ref, cw_ref, gp_ref, bd_ref,
                     qkv_ref, gact_ref, ext_ref):
    i = pl.program_id(1)
    n = pl.num_programs(1)
    tr = cur_ref.shape[0]
    ext_ref[0:CONV_HALO, :] = jnp.where(i > 0, prev_ref[...], 0.0)
    ext_ref[CONV_HALO:CONV_HALO + tr, :] = cur_ref[...]
    ext_ref[CONV_HALO + tr:, :] = jnp.where(i < n - 1, next_ref[...], 0.0)
    pad = (CONV_K - 1) // 2
    acc = None
    for j in range(CONV_K):
        term = ext_ref[pl.ds(CONV_HALO - pad + j, tr), :] * cw_ref[j:j + 1, :]
        acc = term if acc is None else acc + term
    y = _silu(acc)
    bd = bd_ref[...]
    dk_scale = HEAD_DIM ** -0.5
    for part, mul in ((0, dk_scale), (1, 1.0)):
        t = y[:, part * A_W:(part + 1) * A_W]
        ss = _dot_exact_rhs(t * t, bd)
        qkv_ref[:, part * A_W:(part + 1) * A_W] = t * (lax.rsqrt(ss + 1e-6) * mul)
    qkv_ref[:, 2 * A_W:] = y[:, 2 * A_W:]
    a = gate_ref[...]
    z = a + gp_ref[1:2, :]
    softplus = jnp.maximum(z, 0.0) + jnp.log(1.0 + jnp.exp(-jnp.abs(z)))
    g = gp_ref[0:1, :] * softplus
    lane = lax.broadcasted_iota(jnp.int32, a.shape, 1)
    gact_ref[...] = jnp.where(lane < GATE_W // 2, g, jnp.where(lane < GATE_W, _sigmoid(a), 0.0))


def _gdn_prep(aqkv, gates, conv_w, a_log, dt_bias, tr=512):
    b, s, w = aqkv.shape
    tr = min(tr, s)
    hb = tr // CONV_HALO
    nblk8 = s // CONV_HALO
    cw = jnp.zeros((8, w), F32).at[:CONV_K].set(conv_w.astype(F32))
    gp = jnp.zeros((8, LANES), F32)
    gp = gp.at[0, :GATE_W // 2].set(-jnp.exp(a_log.astype(F32).reshape(-1)))
    gp = gp.at[1, :GATE_W // 2].set(dt_bias.astype(F32).reshape(-1))
    head = np.arange(A_W) // HEAD_DIM
    bd = jnp.asarray(head[:, None] == head[None, :], BF16)
    const = lambda shape: pl.BlockSpec(shape, lambda bi, i: (0, 0))
    return pl.pallas_call(
        _gdn_prep_kernel,
        grid=(b, s // tr),
        in_specs=[
            pl.BlockSpec((None, CONV_HALO, w), lambda bi, i: (bi, jnp.maximum(i * hb - 1, 0), 0)),
            pl.BlockSpec((None, tr, w), lambda bi, i: (bi, i, 0)),
            pl.BlockSpec((None, CONV_HALO, w), lambda bi, i: (bi, jnp.minimum((i + 1) * hb, nblk8 - 1), 0)),
            pl.BlockSpec((None, tr, LANES), lambda bi, i: (bi, i, 0)),
            const((8, w)), const((8, LANES)), const((A_W, A_W)),
        ],
        out_specs=[pl.BlockSpec((None, tr, w), lambda bi, i: (bi, i, 0)),
                   pl.BlockSpec((None, tr, LANES), lambda bi, i: (bi, i, 0))],
        out_shape=[jax.ShapeDtypeStruct((b, s, w), F32), jax.ShapeDtypeStruct((b, s, LANES), F32)],
        scratch_shapes=[pltpu.VMEM((tr + 2 * CONV_HALO, w), F32)],
        compiler_params=_cparams("parallel", "parallel"),
        name="gdn_prep",
    )(aqkv, aqkv, aqkv, gates, cw, gp, bd)


def _gdn_consts(group):
    c = GDN_CHUNK
    w = A_W
    r = group * c
    i = np.arange(c)[:, None]
    j = np.arange(w)[None, :] % c
    col_head = np.arange(w)[None, :] // c
    row = np.arange(w)[:, None]
    t = np.arange(r)
    same_chunk = (t[:, None] // c) == (t[None, :] // c)
    tot = np.arange(16)[:, None] == (t[None, :] // c)
    consts = {}
    for name, rev in (("f", False), ("r", True)):
        ge = (i <= j) if rev else (i >= j)
        consts["tril_" + name] = ge.astype(np.float32)
        consts["strict_" + name] = (ge & (i != j)).astype(np.float32)
        tri = (t[None, :] >= t[:, None]) if rev else (t[None, :] <= t[:, None])
        consts["cum_" + name] = np.concatenate([tri & same_chunk, tot], axis=0).astype(np.float32)
        upper = (i >= j) if rev else (i <= j)
        consts["upper_" + name] = np.tile(upper, (group, 1)).astype(np.float32)
    blk16 = (i // 16) == (j // 16)
    blk32 = (i // 32) == (j // 32)
    consts["m16"] = blk16.astype(np.float32)
    consts["m32"] = (blk32 & ~blk16).astype(np.float32)
    consts["m64"] = (~blk32).astype(np.float32)
    consts["eye"] = (i == j).astype(np.float32)
    consts["bd"] = ((row // c) == col_head).astype(np.float32)
    consts["bd16"] = consts["bd"]
    for d, name in enumerate(("f", "r")):
        sel = np.zeros((LANES, 2 * w), np.float32)
        for part in range(2):
            for h in range(GDN_HEADS):
                lane = part * 2 * GDN_HEADS + d * GDN_HEADS + h
                sel[lane, part * w + h * c: part * w + (h + 1) * c] = 1.0
        consts["sel_" + name] = sel
    return consts


_GDN_CONST_ORDER = ("tril_f", "strict_f", "cum_f", "upper_f", "sel_f", "tril_r", "strict_r", "cum_r", "upper_r",
                    "sel_r", "m16", "m32", "m64", "eye", "bd", "bd16")
_GDN_BF16_CONSTS = ("cum_f", "cum_r", "sel_f", "sel_r", "bd16")


def _expand(x16, bd16):
    c = GDN_CHUNK
    zero = jnp.zeros((c, LANES), BF16)
    blocks = []
    for h in range(GDN_HEADS):
        tile = (h * c) // LANES
        kept = x16[:, tile * LANES:(tile + 1) * LANES] * bd16[h * c:(h + 1) * c, tile * LANES:(tile + 1) * LANES]
        blocks.append(jnp.concatenate([kept if t == tile else zero for t in range(A_W // LANES)], axis=1))
    return jnp.concatenate(blocks, axis=0)


def _gdn_chains_prep(chains, cst):
    c = GDN_CHUNK
    w = A_W
    bd16 = cst["bd16"]
    eye = cst["eye"]
    for ch in chains:
        kq = _dot_nt(jnp.concatenate([ch["kb"], ch["q"]], axis=0).astype(BF16), _expand(ch["k"].astype(BF16), bd16))
        ch["low"] = kq[0:c] * ch["decay"] * cst["strict_" + ch["name"]]
        ch["intra"] = (kq[c:2 * c] * ch["decay"] * cst["tril_" + ch["name"]]).astype(BF16)
    for ch in chains:
        n1 = -(ch["low"] * cst["m16"])
        ch["p"] = eye + n1
        ch["nb"] = n1.astype(BF16)
    for ch in chains:
        ch["nb"] = _dot(ch["nb"], _expand(ch["nb"], bd16)).astype(BF16)
    for _ in range(2):
        for ch in chains:
            r = _dot(jnp.concatenate([ch["nb"], ch["p"].astype(BF16)], axis=0), _expand(ch["nb"], bd16))
            ch["nb"] = r[0:c].astype(BF16)
            ch["p"] = ch["p"] + r[c:2 * c]
    for ch in chains:
        ch["inv"] = ch["p"] + _dot(ch["p"].astype(BF16), _expand(ch["nb"], bd16))
    for mname in ("m32", "m64"):
        for ch in chains:
            ch["invb"] = ch["inv"].astype(BF16)
            ch["t1"] = _dot((ch["low"] * cst[mname]).astype(BF16), _expand(ch["invb"], bd16)).astype(BF16)
        for ch in chains:
            ch["inv"] = ch["inv"] - _dot(ch["invb"], _expand(ch["t1"], bd16))
    for ch in chains:
        rhs = jnp.concatenate([_expand(ch["vb"].astype(BF16), bd16), _expand(ch["kbg"].astype(BF16), bd16)], axis=1)
        uk = _dot(ch["inv"].astype(BF16), rhs)
        ch["u"] = uk[:, 0:w]
        ch["kq_lhs"] = jnp.concatenate([uk[:, w:2 * w], ch["qg"]], axis=0).astype(BF16)


def _gdn_scan_step(states, chs, cst):
    c = GDN_CHUNK
    ks = [_dot(ch["kq_lhs"], st.astype(BF16)) for st, ch in zip(states, chs)]
    v16 = [(ch["u"] - k_[0:c]).astype(BF16) for k_, ch in zip(ks, chs)]
    upd = [_dot_tn(ch["kd"], v) for v, ch in zip(v16, chs)]
    outs = [k_[c:2 * c] + _dot(ch["intra"], _expand(v, cst["bd16"])) for k_, v, ch in zip(ks, v16, chs)]
    states = [st * ch["egl"] + u_ * cst["bd"] for st, u_, ch in zip(states, upd, chs)]
    return states, outs


def _gdn_block_gates(qkv_ref, gact_ref, cst, name, group):
    c = GDN_CHUNK
    w = A_W
    r = group * c
    qkv = qkv_ref[...]
    q = qkv[:, 0:w]
    k = qkv[:, w:2 * w]
    v = qkv[:, 2 * w:3 * w]
    gsel = _dot_exact_rhs(gact_ref[...], cst["sel_" + name])
    gb = gsel[:, 0:w]
    bb = gsel[:, w:2 * w]
    cum = _dot_exact_lhs(cst["cum_" + name], gb)
    gc = cum[0:r]
    gl_rows = cum[r:]
    gr_rows = _dot_exact_lhs(cst["cum_" + name][r:], gb * cst["upper_" + name])
    kb = k * bb
    vb = v * bb
    eg = jnp.exp(gc)
    kbg = kb * eg
    qg = q * eg
    chunks = []
    for g in range(group):
        rows = slice(g * c, (g + 1) * c)
        gl = gl_rows[g:g + 1]
        chunks.append(dict(name=name, q=q[rows], k=k[rows], kb=kb[rows], vb=vb[rows], kbg=kbg[rows],
                           decay=jnp.exp(jnp.minimum(gc[rows] - gr_rows[g:g + 1], 0.0)), qg=qg[rows],
                           kd=(k[rows] * jnp.exp(gl - gc[rows])).astype(BF16), egl=jnp.exp(gl)))
    return chunks


def _gdn_scan_kernel(*refs, group):
    nconst = len(_GDN_CONST_ORDER)
    qkv_f_ref, gact_f_ref, qkv_r_ref, gact_r_ref = refs[0:4]
    cst = {n: r[...] for n, r in zip(_GDN_CONST_ORDER, refs[4:4 + nconst])}
    of_ref, or_ref, sf_ref, sr_ref = refs[4 + nconst:]

    @pl.when(pl.program_id(1) == 0)
    def _():
        sf_ref[...] = jnp.zeros_like(sf_ref)
        sr_ref[...] = jnp.zeros_like(sr_ref)

    c = GDN_CHUNK
    chunks_f = _gdn_block_gates(qkv_f_ref, gact_f_ref, cst, "f", group)
    chunks_r = _gdn_block_gates(qkv_r_ref, gact_r_ref, cst, "r", group)
    _gdn_chains_prep(chunks_f + chunks_r, cst)
    states = [sf_ref[...], sr_ref[...]]
    for step in range(group):
        gf = step
        gr = group - 1 - step
        states, (out_f, out_r) = _gdn_scan_step(states, [chunks_f[gf], chunks_r[gr]], cst)
        of_ref[gf * c:(gf + 1) * c, :] = out_f
        or_ref[gr * c:(gr + 1) * c, :] = out_r
    sf_ref[...] = states[0]
    sr_ref[...] = states[1]


def _gdn_scan(qkv, gact, group=GDN_GROUP):
    b, s, w3 = qkv.shape
    group = min(group, s // GDN_CHUNK)
    r = group * GDN_CHUNK
    nblk = s // r
    consts = _gdn_consts(group)
    const_arrays = [jnp.asarray(consts[n], BF16 if n in _GDN_BF16_CONSTS else F32) for n in _GDN_CONST_ORDER]
    fwd = lambda bi, ci: (bi, ci, 0)
    rev = lambda bi, ci: (bi, nblk - 1 - ci, 0)
    in_specs = [pl.BlockSpec((None, r, w3), fwd), pl.BlockSpec((None, r, LANES), fwd),
                pl.BlockSpec((None, r, w3), rev), pl.BlockSpec((None, r, LANES), rev)]
    in_specs += [pl.BlockSpec(a.shape, lambda bi, ci: (0, 0)) for a in const_arrays]
    return pl.pallas_call(
        functools.partial(_gdn_scan_kernel, group=group),
        grid=(b, nblk),
        in_specs=in_specs,
        out_specs=[pl.BlockSpec((None, r, A_W), fwd), pl.BlockSpec((None, r, A_W), rev)],
        out_shape=[jax.ShapeDtypeStruct((b, s, A_W), F32)] * 2,
        scratch_shapes=[pltpu.VMEM((A_W, A_W), F32)] * 2,
        compiler_params=_cparams("parallel", "arbitrary"),
        name="gdn_scan",
    )(qkv, gact, qkv, gact, *const_arrays)


MIN_DENOM = 2.0 ** -80
ATTN_BLOCKS_PER_ITER = 2
SHIFT_LANES = 3
KEY_BLOCK = 256
VT_ROWS = LANES + 16


def _augment_kv(k_ref, v_ref, kaug_ref, vt_ref):
    s = k_ref.shape[0]
    lane = lax.broadcasted_iota(jnp.int32, (s, LANES), 1)
    k = k_ref[...]
    kaug_ref[:, 0:LANES] = k
    kaug_ref[:, LANES:] = jnp.where(lane < SHIFT_LANES, 1.0, 0.0).astype(BF16)
    for c in range(s // KEY_BLOCK):
        vt_ref[c, 0:LANES, :] = v_ref[c * KEY_BLOCK:(c + 1) * KEY_BLOCK, :].astype(F32).T.astype(BF16)
        vt_ref[c, LANES:, :] = jnp.ones((VT_ROWS - LANES, KEY_BLOCK), BF16)
    kk = k.astype(F32)
    kk = kk * kk
    norms = []
    for half in range(2):
        n2 = jnp.sum(jnp.where((lane // HEAD_DIM) == half, kk, 0.0), axis=-1, keepdims=True)
        norms.append(jnp.max(n2, axis=0, keepdims=True))
    return norms


def _stack_halves(q):
    lane = lax.broadcasted_iota(jnp.int32, q.shape, 1)
    zero = jnp.zeros_like(q)
    return jnp.concatenate([jnp.where(lane < HEAD_DIM, q, zero), jnp.where(lane >= HEAD_DIM, q, zero)], axis=0)


def _score_bound(qs, knorms):
    rows = qs.shape[0]
    qf = qs.astype(F32)
    q2 = jnp.sum(qf * qf, axis=-1, keepdims=True)
    row = lax.broadcasted_iota(jnp.int32, (rows, 1), 0)
    k2 = jnp.where(row < rows // 2, knorms[0], knorms[1])
    return jnp.sqrt(q2 * k2) * 1.001 + 1e-30


def _augment_q(qs, shift):
    rows = qs.shape[0]
    hi, mid, lo = (t.astype(F32) for t in _split3(-shift))
    lane = lax.broadcasted_iota(jnp.int32, (rows, LANES), 1)
    m_cols = jnp.where(lane == 0, hi, jnp.where(lane == 1, mid, jnp.where(lane == 2, lo, 0.0)))
    return jnp.concatenate([qs, m_cols.astype(BF16)], axis=1)


def _softmax_pv(q_aug, kaug_ref, vt_ref, key0, nchunk, kc, unroll, weight_fn=None, clamp=False):
    queries = q_aug.shape[0]
    kb = kc // KEY_BLOCK

    def scores(j):
        start = pl.multiple_of(key0 + j * kc, KEY_BLOCK)
        return _dot_nt(kaug_ref[pl.ds(start, kc), :], q_aug)

    def weighted_values(j, sc):
        blk0 = key0 // KEY_BLOCK + j * kb
        p = jnp.exp2(jnp.minimum(sc, 0.0) if clamp else sc)
        if weight_fn is not None:
            p = p * weight_fn(j)
        vt = jnp.concatenate([vt_ref[blk0 + t] for t in range(kb)], axis=1) if kb > 1 else vt_ref[blk0]
        return _dot(vt, p.astype(BF16))

    if unroll is None:
        acc = None
        sc = scores(0)
        for j in range(nchunk):
            sc_next = scores(j + 1) if j + 1 < nchunk else None
            part = weighted_values(j, sc)
            acc = part if acc is None else acc + part
            sc = sc_next
        return acc

    def body(j, acc):
        return acc + weighted_values(j, scores(j))

    return lax.fori_loop(0, nchunk, body, jnp.zeros((VT_ROWS, queries), F32), unroll=unroll)


def _score_max(qs, k_ref, key0, nchunk, kc, weight_fn=None):
    queries = qs.shape[0]

    def body(j, m8):
        start = pl.multiple_of(key0 + j * kc, KEY_BLOCK)
        sc = _dot_nt(k_ref[pl.ds(start, kc), :], qs)
        if weight_fn is not None:
            sc = jnp.where(weight_fn(j) > 0.0, sc, NEG_BIG)
        return jnp.maximum(m8, jnp.max(sc.reshape(kc // 8, 8, queries), axis=0))

    m8 = lax.fori_loop(0, nchunk, body, jnp.full((8, queries), NEG_BIG, F32))
    m_rows = jnp.broadcast_to(jnp.max(m8, axis=0, keepdims=True), (LANES, queries))
    return m_rows.T[:, 0:1]


def _attention_blocks(nblk, block_fn, finish_fn, knorms, k_ref, kaug_ref, vt_ref):
    per_iter = ATTN_BLOCKS_PER_ITER if nblk % ATTN_BLOCKS_PER_ITER == 0 else 1

    def fast(ii, lmin):
        blocks = []
        for t in range(per_iter):
            i = ii * per_iter + t
            qs, key0, nchunk, kc, weight_fn = block_fn(i)
            blocks.append((i, _augment_q(qs, _score_bound(qs, knorms)), key0, nchunk, kc, weight_fn))
        for i, q_aug, key0, nchunk, kc, weight_fn in blocks:
            acc = _softmax_pv(q_aug, kaug_ref, vt_ref, key0, nchunk, kc, None, weight_fn)
            finish_fn(i, acc)
            den = acc[LANES:LANES + 8, :]
            for c0 in range(0, den.shape[1], LANES):
                lmin = jnp.minimum(lmin, den[:, c0:c0 + LANES])
        return lmin

    lmin = lax.fori_loop(0, nblk // per_iter, fast, jnp.full((8, LANES), 3e38, F32))
    all_ok = jnp.min(lmin) >= MIN_DENOM

    @pl.when(jnp.logical_not(all_ok))
    def _():
        def exact(i, carry):
            qs, key0, nchunk, kc, weight_fn = block_fn(i)
            m = _score_max(qs, k_ref, key0, nchunk, kc, weight_fn)
            finish_fn(i, _softmax_pv(_augment_q(qs, m), kaug_ref, vt_ref, key0, nchunk, kc, 1, weight_fn,
                                     clamp=weight_fn is not None))
            return carry

        lax.fori_loop(0, nblk, exact, 0)


def _dilated_counts(tq):
    r = DIL_REACH
    ii = np.arange(tq)[:, None]
    m = np.arange(tq + 4 * r)[None, :]
    d = 2 * r + ii - m
    count = np.zeros(d.shape, np.float32)
    for window, dil in DIL_PATTERNS:
        count += ((d % dil) == 0) & (np.abs(d) <= window // 2)
    return jnp.asarray(count.reshape(tq, -1, tq).transpose(1, 2, 0))


def _dilated_kernel(q_ref, k_ref, v_ref, cnt_ref, o_ref, kaug_ref, vt_ref, *, tq):
    s = k_ref.shape[0]
    nwin = 1 + 2 * DIL_REACH // tq
    knorms = _augment_kv(k_ref, v_ref, kaug_ref, vt_ref)

    def block(i):
        q0 = pl.multiple_of(i * tq, tq)
        ws = jnp.clip(q0 - DIL_REACH, 0, s - nwin * tq)
        mb0 = (2 * DIL_REACH - (q0 - ws)) // tq

        def weights(j):
            cnt = cnt_ref[mb0 + j]
            return jnp.concatenate([cnt, cnt], axis=1)

        qs = _stack_halves(q_ref[pl.ds(q0, tq), :])
        return qs, ws, nwin, tq, weights

    def finish(i, acc):
        row = lax.broadcasted_iota(jnp.int32, (LANES, tq), 0)
        out_t = jnp.where(row < HEAD_DIM, acc[0:LANES, 0:tq] / acc[LANES:LANES + 1, 0:tq],
                          acc[0:LANES, tq:] / acc[LANES:LANES + 1, tq:])
        o_ref[pl.ds(pl.multiple_of(i * tq, tq), tq), :] = out_t.T.astype(o_ref.dtype)

    _attention_blocks(s // tq, block, finish, knorms, k_ref, kaug_ref, vt_ref)


def _attn_scratch(s):
    return [pltpu.VMEM((s, 2 * LANES), BF16), pltpu.VMEM((s // KEY_BLOCK, VT_ROWS, KEY_BLOCK), BF16)]


def _dilated(bqkv):
    b, s, _ = bqkv.shape
    tq = DIL_TQ
    npair = B_W // LANES
    counts = _dilated_counts(tq)
    slab = lambda off: pl.BlockSpec((None, s, LANES), lambda bi, p: (bi, 0, off + p))
    return pl.pallas_call(
        functools.partial(_dilated_kernel, tq=tq),
        grid=(b, npair),
        in_specs=[slab(0), slab(npair), slab(2 * npair), pl.BlockSpec(counts.shape, lambda bi, p: (0, 0, 0))],
        out_specs=slab(0),
        out_shape=jax.ShapeDtypeStruct((b, s, B_W), BF16),
        scratch_shapes=_attn_scratch(s),
        compiler_params=_cparams("parallel", "parallel"),
        name="dilated_attn",
    )(bqkv, bqkv, bqkv, counts)


def _diff_kernel(lam_ref, q_ref, k_ref, v_ref, sw_ref, o_ref, kaug_ref, vt_ref, *, lambda_init, tq, kc):
    s = k_ref.shape[0]
    knorms = _augment_kv(k_ref, v_ref, kaug_ref, vt_ref)
    lp = lam_ref[...]
    lam = (jnp.exp(jnp.sum(lp[0:1] * lp[1:2], axis=-1, keepdims=True))
           - jnp.exp(jnp.sum(lp[2:3] * lp[3:4], axis=-1, keepdims=True)) + lambda_init)
    sw = jnp.concatenate([sw_ref[...]] * (tq // LANES), axis=1)

    def block(i):
        qs = _stack_halves(q_ref[pl.ds(pl.multiple_of(i * tq, tq), tq), :])
        return qs, 0, s // kc, kc, None

    def finish(i, acc):
        o = (acc[0:LANES, 0:tq] / acc[LANES:LANES + 1, 0:tq]
             - lam * (acc[0:LANES, tq:] / acc[LANES:LANES + 1, tq:]))
        o = o * lax.rsqrt(jnp.mean(o * o, axis=0, keepdims=True) + NORM_EPS) * sw
        o_ref[pl.ds(pl.multiple_of(i * tq, tq), tq), :] = (o * (1.0 - lambda_init)).T.astype(o_ref.dtype)

    _attention_blocks(s // tq, block, finish, knorms, k_ref, kaug_ref, vt_ref)


def _diff_attn(cq, ck, cv, lam_params, subln_w, lambda_init):
    b, s, _ = cq.shape
    tq = min(DIFF_TQ, s)
    kc = min(DIFF_KC, s)
    slab = pl.BlockSpec((None, s, LANES), lambda bi, h: (bi, 0, h))
    sw_rows = jnp.broadcast_to(subln_w.astype(F32)[:, None], (LANES, LANES))
    return pl.pallas_call(
        functools.partial(_diff_kernel, lambda_init=lambda_init, tq=tq, kc=kc),
        grid=(b, DIFF_HEADS),
        in_specs=[pl.BlockSpec((8, LANES), lambda bi, h: (0, 0)), slab, slab, slab,
                  pl.BlockSpec((LANES, LANES), lambda bi, h: (0, 0))],
        out_specs=slab,
        out_shape=jax.ShapeDtypeStruct((b, s, C_W), BF16),
        scratch_shapes=_attn_scratch(s),
        compiler_params=_cparams("parallel", "parallel"),
        name="diff_attn",
    )(lam_params, cq, ck, cv, sw_rows)


def _outproj_kernel(x_ref, of_ref, or_ref, z_ref, ob_ref, oc_ref, gw_ref, bd_ref, w_ref, y_ref):
    o = of_ref[...] + or_ref[...]
    ms = _dot_exact_rhs(o * o, bd_ref[...]) * (1.0 / HEAD_DIM)
    oa = o * lax.rsqrt(ms + NORM_EPS) * gw_ref[...] * _silu(z_ref[...])
    acc = x_ref[...] + _dot(oa.astype(BF16), w_ref[0:A_W, :])
    acc += _dot(ob_ref[...], w_ref[A_W:A_W + B_W, :])
    acc += _dot(oc_ref[...], w_ref[A_W + B_W:, :])
    y_ref[...] = acc


def _outproj(x2d, o_f, o_r, z, o_b, o_c, gdn_norm_w, w_out, tm=512):
    t, d = x2d.shape
    tm = min(tm, t)
    rows = lambda width: pl.BlockSpec((tm, width), lambda i: (i, 0))
    const = lambda shape: pl.BlockSpec(shape, lambda i: (0, 0))
    head = np.arange(A_W) // HEAD_DIM
    bd = jnp.asarray(head[:, None] == head[None, :], BF16)
    gw = jnp.tile(gdn_norm_w.astype(F32), GDN_HEADS)[None, :]
    return pl.pallas_call(
        _outproj_kernel,
        grid=(t // tm,),
        in_specs=[rows(d), rows(A_W), rows(A_W), rows(A_W), rows(B_W), rows(C_W),
                  const((1, A_W)), const((A_W, A_W)), const(w_out.shape)],
        out_specs=rows(d),
        out_shape=jax.ShapeDtypeStruct((t, d), F32),
        compiler_params=_cparams("parallel"),
        name="outproj",
    )(x2d, o_f, o_r, z, o_b, o_c, gw, bd, w_out)


def _ffn_kernel(x_ref, nw_ref, wg_ref, wu_ref, wd_ref, fw_ref, y_ref, *, fc, final_norm):
    x = x_ref[...]
    h = (x * lax.rsqrt(jnp.mean(x * x, axis=-1, keepdims=True) + NORM_EPS) * nw_ref[...]).astype(BF16)
    acc = x
    for c0 in range(0, wg_ref.shape[1], fc):
        g = _dot(h, wg_ref[:, c0:c0 + fc])
        u = _dot(h, wu_ref[:, c0:c0 + fc])
        acc = acc + _dot((_silu(g) * u).astype(BF16), wd_ref[c0:c0 + fc, :])
    if final_norm:
        acc = acc * lax.rsqrt(jnp.mean(acc * acc, axis=-1, keepdims=True) + NORM_EPS) * fw_ref[...]
    y_ref[...] = acc


def _ffn(x2d, norm_w, wg, wu, wd, final_w, final_norm, tm=512, fc=256):
    t, d = x2d.shape
    tm = min(tm, t)
    f = wg.shape[1]
    rows = pl.BlockSpec((tm, d), lambda i: (i, 0))
    const = lambda shape: pl.BlockSpec(shape, lambda i: (0, 0), pipeline_mode=pl.Buffered(1))
    return pl.pallas_call(
        functools.partial(_ffn_kernel, fc=fc, final_norm=final_norm),
        grid=(t // tm,),
        in_specs=[rows, const((1, d)), const((d, f)), const((d, f)), const((f, d)), const((1, d))],
        out_specs=rows,
        out_shape=jax.ShapeDtypeStruct((t, d), F32),
        compiler_params=_cparams("parallel"),
        name="ffn",
    )(x2d, norm_w.reshape(1, d).astype(F32), wg, wu, wd, final_w.reshape(1, d).astype(F32))


def _pad_in_weight(w_in):
    gate_end = 4 * A_W + GATE_W
    d = w_in.shape[0]
    return jnp.concatenate([w_in[:, :gate_end], jnp.zeros((d, LANES - GATE_W), w_in.dtype),
                            w_in[:, gate_end:]], axis=1).astype(BF16)


def _layer(x2d, b, s, cos, sin, attn_norm_w, w_in, conv_w, a_log, dt_bias, gdn_norm_w,
           lq1, lk1, lq2, lk2, subln_w, w_out, ffn_norm_w, w_gate, w_up, w_down,
           final_w, lambda_init, last):
    aqkv, az, gates, bqkv, cq, ck, cv = _norm_inproj(x2d, attn_norm_w.astype(F32), _pad_in_weight(w_in), cos, sin)
    qkv, gact = _gdn_prep(aqkv.reshape(b, s, -1), gates.reshape(b, s, -1), conv_w, a_log, dt_bias)
    o_f, o_r = _gdn_scan(qkv, gact)
    o_b = _dilated(bqkv.reshape(b, s, -1))
    lam_params = jnp.zeros((8, LANES), F32).at[0:4, :HEAD_DIM].set(
        jnp.stack([lq1, lk1, lq2, lk2]).astype(F32))
    o_c = _diff_attn(cq.reshape(b, s, -1), ck.reshape(b, s, -1), cv.reshape(b, s, -1),
                     lam_params, subln_w, lambda_init)
    t = b * s
    x2d = _outproj(x2d, o_f.reshape(t, -1), o_r.reshape(t, -1), az, o_b.reshape(t, -1),
                   o_c.reshape(t, -1), gdn_norm_w, w_out.astype(BF16))
    return _ffn(x2d, ffn_norm_w, w_gate.astype(BF16), w_up.astype(BF16), w_down.astype(BF16),
                final_w, last)


def kernel(x, positions, attn_norm_w, w_in, conv_w, a_log, dt_bias, gdn_norm_w, lambda_q1, lambda_k1,
           lambda_q2, lambda_k2, subln_w, w_out, ffn_norm_w, w_gate, w_up, w_down, final_norm_w):
    b, s, d = x.shape
    depth = w_in.shape[0]
    cos, sin = _rope_tables(positions)
    x2d = x.reshape(b * s, d)
    for l in range(depth):
        lambda_init = 0.8 - 0.6 * math.exp(-0.3 * l)
        x2d = _layer(x2d, b, s, cos, sin, attn_norm_w[l], w_in[l], conv_w[l], a_log[l], dt_bias[l],
                     gdn_norm_w[l], lambda_q1[l], lambda_k1[l], lambda_q2[l], lambda_k2[l], subln_w[l],
                     w_out[l], ffn_norm_w[l], w_gate[l], w_up[l], w_down[l], final_norm_w,
                     lambda_init, l == depth - 1)
    return x2d.reshape(b, s, d)
```

```python
import functools
import math

import numpy as np
import jax
import jax.numpy as jnp
from jax import lax
from jax.experimental import pallas as pl
from jax.experimental.pallas import tpu as pltpu

F32 = jnp.float32
BF16 = jnp.bfloat16

NORM_EPS = 1e-6
ROPE_THETA = 10000.0
HEAD_DIM = 64
LANES = 128

GDN_HEADS = 4
GDN_CHUNK = 64
GDN_GROUP = 4
CONV_K = 5
CONV_HALO = 8

DIL_HEADS = 4
DIL_PATTERNS = ((128, 1), (512, 4), (2048, 16))
DIL_REACH = max(w // 2 for w, _ in DIL_PATTERNS)
DIL_TQ = 256
DIL_BLOCKS_PER_ITER = 8
DIL_BPC = 1

DIFF_HEADS = 4
DIFF_TQ = 256
DIFF_KC = 512
DIFF_BLOCKS_PER_ITER = 4

A_W = GDN_HEADS * HEAD_DIM
B_W = DIL_HEADS * HEAD_DIM
C_W = DIFF_HEADS * 2 * HEAD_DIM
GATE_W = 2 * 2 * GDN_HEADS
NEG_BIG = -1e30
LOG2E = math.log2(math.e)

VMEM_LIMIT = 56 * 1024 * 1024


def _cparams(*sem):
    return pltpu.CompilerParams(dimension_semantics=sem, vmem_limit_bytes=VMEM_LIMIT)


def _split3(x):
    hi = x.astype(BF16)
    r1 = x - hi.astype(F32)
    mid = r1.astype(BF16)
    lo = (r1 - mid.astype(F32)).astype(BF16)
    return hi, mid, lo


def _dot(a, b):
    return jnp.dot(a, b, preferred_element_type=F32)


def _dot_nt(a, b):
    return lax.dot_general(a, b, (((1,), (1,)), ((), ())), preferred_element_type=F32)


def _dot_tn(a, b):
    return lax.dot_general(a, b, (((0,), (0,)), ((), ())), preferred_element_type=F32)


def _dot_exact_rhs(x, m_bf16):
    hi, mid, lo = _split3(x)
    return _dot(hi, m_bf16) + _dot(mid, m_bf16) + _dot(lo, m_bf16)


def _dot_exact_lhs(m_bf16, x):
    hi, mid, lo = _split3(x)
    return _dot(m_bf16, hi) + _dot(m_bf16, mid) + _dot(m_bf16, lo)


def _sigmoid(x):
    return 1.0 / (1.0 + jnp.exp(-x))


def _silu(x):
    return x * _sigmoid(x)


def _rope_kernel(pos_ref, inv_ref, sign_ref, cos_ref, sin_ref):
    ang = pos_ref[...].astype(F32) * inv_ref[...]
    cos_ref[...] = jnp.cos(ang)
    sin_ref[...] = jnp.sin(ang) * sign_ref[...]


def _rope_tables(positions):
    t = positions.size
    tr = min(t, 1024)
    half = HEAD_DIM // 2
    inv = ROPE_THETA ** (-jnp.arange(0, HEAD_DIM, 2, dtype=F32) / HEAD_DIM)
    inv_row = jnp.tile(inv, LANES // half)[None, :]
    sign_row = jnp.asarray(np.where((np.arange(LANES) % HEAD_DIM) < half, -1.0, 1.0), F32)[None, :]
    row = pl.BlockSpec((1, LANES), lambda i: (0, 0))
    out = pl.BlockSpec((tr, LANES), lambda i: (i, 0))
    return pl.pallas_call(
        _rope_kernel,
        grid=(t // tr,),
        in_specs=[pl.BlockSpec((tr, 1), lambda i: (i, 0)), row, row],
        out_specs=[out, out],
        out_shape=[jax.ShapeDtypeStruct((t, LANES), F32)] * 2,
        compiler_params=_cparams("parallel"),
        name="rope_tables",
    )(positions.reshape(t, 1), inv_row, sign_row)


def _rope(y, cos, sin):
    half = HEAD_DIM // 2
    lane = lax.broadcasted_iota(jnp.int32, cos.shape, 1)
    first_half = (lane % HEAD_DIM) < half
    slabs = []
    for c0 in range(0, y.shape[1], LANES):
        ys = y[:, c0:c0 + LANES]
        partner = jnp.where(first_half, pltpu.roll(ys, LANES - half, 1), pltpu.roll(ys, half, 1))
        slabs.append(ys * cos + partner * sin)
    return jnp.concatenate(slabs, axis=1)


def _inproj_kernel(x_ref, nw_ref, w_ref, cos_ref, sin_ref,
                   aqkv_ref, az_ref, gate_ref, bqkv_ref, cq_ref, ck_ref, cv_ref):
    x = x_ref[...]
    h = x * lax.rsqrt(jnp.mean(x * x, axis=-1, keepdims=True) + NORM_EPS) * nw_ref[...]
    h = h.astype(BF16)
    cos = cos_ref[...]
    sin = sin_ref[...]
    scale = HEAD_DIM ** -0.5 * LOG2E

    def proj(start, width):
        return _dot(h, w_ref[:, start:start + width])

    o = 0
    aqkv_ref[...] = proj(o, 3 * A_W)
    o += 3 * A_W
    az_ref[...] = proj(o, A_W)
    o += A_W
    gate_ref[...] = proj(o, LANES)
    o += LANES
    bqkv_ref[:, 0:B_W] = (_rope(proj(o, B_W), cos, sin) * scale).astype(BF16)
    o += B_W
    bqkv_ref[:, B_W:2 * B_W] = _rope(proj(o, B_W), cos, sin).astype(BF16)
    o += B_W
    bqkv_ref[:, 2 * B_W:3 * B_W] = proj(o, B_W).astype(BF16)
    o += B_W
    cq_ref[...] = (_rope(proj(o, C_W), cos, sin) * scale).astype(BF16)
    o += C_W
    ck_ref[...] = _rope(proj(o, C_W), cos, sin).astype(BF16)
    o += C_W
    cv_ref[...] = proj(o, C_W).astype(BF16)


def _norm_inproj(x2d, norm_w, w_pad, cos, sin, tm=512):
    t, d = x2d.shape
    tm = min(tm, t)
    np_ = w_pad.shape[1]
    rows = lambda width: pl.BlockSpec((tm, width), lambda i: (i, 0))
    out_w = (3 * A_W, A_W, LANES, 3 * B_W, C_W, C_W, C_W)
    out_dt = (F32, F32, F32, BF16, BF16, BF16, BF16)
    return pl.pallas_call(
        _inproj_kernel,
        grid=(t // tm,),
        in_specs=[rows(d), pl.BlockSpec((1, d), lambda i: (0, 0)),
                  pl.BlockSpec((d, np_), lambda i: (0, 0)), rows(LANES), rows(LANES)],
        out_specs=[rows(w) for w in out_w],
        out_shape=[jax.ShapeDtypeStruct((t, w), dt) for w, dt in zip(out_w, out_dt)],
        compiler_params=_cparams("parallel"),
        name="norm_inproj",
    )(x2d, norm_w.reshape(1, d), w_pad, cos, sin)


def _gdn_prep_kernel(prev_ref, cur_ref, next_ref, gate_ref, cw_ref, gp_ref, bd_ref,
                     qkv_ref, gact_ref, ext_ref):
    i = pl.program_id(1)
    n = pl.num_programs(1)
    tr = cur_ref.shape[0]
    ext_ref[0:CONV_HALO, :] = jnp.where(i > 0, prev_ref[...], 0.0)
    ext_ref[CONV_HALO:CONV_HALO + tr, :] = cur_ref[...]
    ext_ref[CONV_HALO + tr:, :] = jnp.where(i < n - 1, next_ref[...], 0.0)
    pad = (CONV_K - 1) // 2
    acc = None
    for j in range(CONV_K):
        term = ext_ref[pl.ds(CONV_HALO - pad + j, tr), :] * cw_ref[j:j + 1, :]
        acc = term if acc is None else acc + term
    y = _silu(acc)
    bd = bd_ref[...]
    dk_scale = HEAD_DIM ** -0.5
    for part, mul in ((0, dk_scale), (1, 1.0)):
        t = y[:, part * A_W:(part + 1) * A_W]
        ss = _dot_exact_rhs(t * t, bd)
        qkv_ref[:, part * A_W:(part + 1) * A_W] = t * (lax.rsqrt(ss + 1e-6) * mul)
    qkv_ref[:, 2 * A_W:] = y[:, 2 * A_W:]
    a = gate_ref[...]
    z = a + gp_ref[1:2, :]
    softplus = jnp.maximum(z, 0.0) + jnp.log(1.0 + jnp.exp(-jnp.abs(z)))
    g = gp_ref[0:1, :] * softplus
    lane = lax.broadcasted_iota(jnp.int32, a.shape, 1)
    gact_ref[...] = jnp.where(lane < GATE_W // 2, g, jnp.where(lane < GATE_W, _sigmoid(a), 0.0))


def _gdn_prep(aqkv, gates, conv_w, a_log, dt_bias, tr=512):
    b, s, w = aqkv.shape
    tr = min(tr, s)
    hb = tr // CONV_HALO
    nblk8 = s // CONV_HALO
    cw = jnp.zeros((8, w), F32).at[:CONV_K].set(conv_w.astype(F32))
    gp = jnp.zeros((8, LANES), F32)
    gp = gp.at[0, :GATE_W // 2].set(-jnp.exp(a_log.astype(F32).reshape(-1)))
    gp = gp.at[1, :GATE_W // 2].set(dt_bias.astype(F32).reshape(-1))
    head = np.arange(A_W) // HEAD_DIM
    bd = jnp.asarray(head[:, None] == head[None, :], BF16)
    const = lambda shape: pl.BlockSpec(shape, lambda bi, i: (0, 0))
    return pl.pallas_call(
        _gdn_prep_kernel,
        grid=(b, s // tr),
        in_specs=[
            pl.BlockSpec((None, CONV_HALO, w), lambda bi, i: (bi, jnp.maximum(i * hb - 1, 0), 0)),
            pl.BlockSpec((None, tr, w), lambda bi, i: (bi, i, 0)),
            pl.BlockSpec((None, CONV_HALO, w), lambda bi, i: (bi, jnp.minimum((i + 1) * hb, nblk8 - 1), 0)),
            pl.BlockSpec((None, tr, LANES), lambda bi, i: (bi, i, 0)),
            const((8, w)), const((8, LANES)), const((A_W, A_W)),
        ],
        out_specs=[pl.BlockSpec((None, tr, w), lambda bi, i: (bi, i, 0)),
                   pl.BlockSpec((None, tr, LANES), lambda bi, i: (bi, i, 0))],
        out_shape=[jax.ShapeDtypeStruct((b, s, w), F32), jax.ShapeDtypeStruct((b, s, LANES), F32)],
        scratch_shapes=[pltpu.VMEM((tr + 2 * CONV_HALO, w), F32)],
        compiler_params=_cparams("parallel", "parallel"),
        name="gdn_prep",
    )(aqkv, aqkv, aqkv, gates, cw, gp, bd)


def _gdn_consts(group):
    c = GDN_CHUNK
    w = A_W
    r = group * c
    i = np.arange(c)[:, None]
    j = np.arange(w)[None, :] % c
    col_head = np.arange(w)[None, :] // c
    row = np.arange(w)[:, None]
    t = np.arange(r)
    same_chunk = (t[:, None] // c) == (t[None, :] // c)
    tot = np.arange(16)[:, None] == (t[None, :] // c)
    consts = {}
    for name, rev in (("f", False), ("r", True)):
        ge = (i <= j) if rev else (i >= j)
        consts["tril_" + name] = ge.astype(np.float32)
        consts["strict_" + name] = (ge & (i != j)).astype(np.float32)
        tri = (t[None, :] >= t[:, None]) if rev else (t[None, :] <= t[:, None])
        consts["cum_" + name] = np.concatenate([tri & same_chunk, tot], axis=0).astype(np.float32)
        upper = (i >= j) if rev else (i <= j)
        consts["upper_" + name] = np.tile(upper, (group, 1)).astype(np.float32)
    blk16 = (i // 16) == (j // 16)
    blk32 = (i // 32) == (j // 32)
    consts["m16"] = blk16.astype(np.float32)
    consts["m32"] = (blk32 & ~blk16).astype(np.float32)
    consts["m64"] = (~blk32).astype(np.float32)
    consts["eye"] = (i == j).astype(np.float32)
    consts["bd"] = ((row // c) == col_head).astype(np.float32)
    consts["bd16"] = consts["bd"]
    for d, name in enumerate(("f", "r")):
        sel = np.zeros((LANES, 2 * w), np.float32)
        for part in range(2):
            for h in range(GDN_HEADS):
                lane = part * 2 * GDN_HEADS + d * GDN_HEADS + h
                sel[lane, part * w + h * c: part * w + (h + 1) * c] = 1.0
        consts["sel_" + name] = sel
    return consts


_GDN_CONST_ORDER = ("tril_f", "strict_f", "cum_f", "upper_f", "sel_f", "tril_r", "strict_r", "cum_r", "upper_r",
                    "sel_r", "m16", "m32", "m64", "eye", "bd", "bd16")
_GDN_BF16_CONSTS = ("cum_f", "cum_r", "sel_f", "sel_r", "bd16")


def _expand(x16, bd16):
    c = GDN_CHUNK
    zero = jnp.zeros((c, LANES), BF16)
    blocks = []
    for h in range(GDN_HEADS):
        tile = (h * c) // LANES
        kept = x16[:, tile * LANES:(tile + 1) * LANES] * bd16[h * c:(h + 1) * c, tile * LANES:(tile + 1) * LANES]
        blocks.append(jnp.concatenate([kept if t == tile else zero for t in range(A_W // LANES)], axis=1))
    return jnp.concatenate(blocks, axis=0)


def _gdn_chains_prep(chains, cst):
    c = GDN_CHUNK
    w = A_W
    bd16 = cst["bd16"]
    eye = cst["eye"]
    for ch in chains:
        kq = _dot_nt(jnp.concatenate([ch["kb"], ch["q"]], axis=0).astype(BF16), _expand(ch["k"].astype(BF16), bd16))
        ch["low"] = kq[0:c] * ch["decay"] * cst["strict_" + ch["name"]]
        ch["intra"] = (kq[c:2 * c] * ch["decay"] * cst["tril_" + ch["name"]]).astype(BF16)
    for ch in chains:
        n1 = -(ch["low"] * cst["m16"])
        ch["p"] = eye + n1
        ch["nb"] = n1.astype(BF16)
    for ch in chains:
        ch["nb"] = _dot(ch["nb"], _expand(ch["nb"], bd16)).astype(BF16)
    for _ in range(2):
        for ch in chains:
            r = _dot(jnp.concatenate([ch["nb"], ch["p"].astype(BF16)], axis=0), _expand(ch["nb"], bd16))
            ch["nb"] = r[0:c].astype(BF16)
            ch["p"] = ch["p"] + r[c:2 * c]
    for ch in chains:
        ch["inv"] = ch["p"] + _dot(ch["p"].astype(BF16), _expand(ch["nb"], bd16))
    for mname in ("m32", "m64"):
        for ch in chains:
            ch["invb"] = ch["inv"].astype(BF16)
            ch["t1"] = _dot((ch["low"] * cst[mname]).astype(BF16), _expand(ch["invb"], bd16)).astype(BF16)
        for ch in chains:
            ch["inv"] = ch["inv"] - _dot(ch["invb"], _expand(ch["t1"], bd16))
    for ch in chains:
        rhs = jnp.concatenate([_expand(ch["vb"].astype(BF16), bd16), _expand(ch["kbg"].astype(BF16), bd16)], axis=1)
        uk = _dot(ch["inv"].astype(BF16), rhs)
        ch["u"] = uk[:, 0:w]
        ch["kq_lhs"] = jnp.concatenate([uk[:, w:2 * w], ch["qg"]], axis=0).astype(BF16)


def _gdn_scan_step(states, chs, cst):
    c = GDN_CHUNK
    ks = [_dot(ch["kq_lhs"], st.astype(BF16)) for st, ch in zip(states, chs)]
    v16 = [(ch["u"] - k_[0:c]).astype(BF16) for k_, ch in zip(ks, chs)]
    upd = [_dot_tn(ch["kd"], v) for v, ch in zip(v16, chs)]
    outs = [k_[c:2 * c] + _dot(ch["intra"], _expand(v, cst["bd16"])) for k_, v, ch in zip(ks, v16, chs)]
    states = [st * ch["egl"] + u_ * cst["bd"] for st, u_, ch in zip(states, upd, chs)]
    return states, outs


def _gdn_block_gates(qkv_ref, gact_ref, cst, name, group):
    c = GDN_CHUNK
    w = A_W
    r = group * c
    qkv = qkv_ref[...]
    q = qkv[:, 0:w]
    k = qkv[:, w:2 * w]
    v = qkv[:, 2 * w:3 * w]
    gsel = _dot_exact_rhs(gact_ref[...], cst["sel_" + name])
    gb = gsel[:, 0:w]
    bb = gsel[:, w:2 * w]
    cum = _dot_exact_lhs(cst["cum_" + name], gb)
    gc = cum[0:r]
    gl_rows = cum[r:]
    gr_rows = _dot_exact_lhs(cst["cum_" + name][r:], gb * cst["upper_" + name])
    kb = k * bb
    vb = v * bb
    eg = jnp.exp(gc)
    kbg = kb * eg
    qg = q * eg
    chunks = []
    for g in range(group):
        rows = slice(g * c, (g + 1) * c)
        gl = gl_rows[g:g + 1]
        chunks.append(dict(name=name, q=q[rows], k=k[rows], kb=kb[rows], vb=vb[rows], kbg=kbg[rows],
                           decay=jnp.exp(jnp.minimum(gc[rows] - gr_rows[g:g + 1], 0.0)), qg=qg[rows],
                           kd=(k[rows] * jnp.exp(gl - gc[rows])).astype(BF16), egl=jnp.exp(gl)))
    return chunks


def _gdn_scan_kernel(*refs, group):
    nconst = len(_GDN_CONST_ORDER)
    qkv_f_ref, gact_f_ref, qkv_r_ref, gact_r_ref = refs[0:4]
    cst = {n: r[...] for n, r in zip(_GDN_CONST_ORDER, refs[4:4 + nconst])}
    of_ref, or_ref, sf_ref, sr_ref = refs[4 + nconst:]

    @pl.when(pl.program_id(1) == 0)
    def _():
        sf_ref[...] = jnp.zeros_like(sf_ref)
        sr_ref[...] = jnp.zeros_like(sr_ref)

    c = GDN_CHUNK
    chunks_f = _gdn_block_gates(qkv_f_ref, gact_f_ref, cst, "f", group)
    chunks_r = _gdn_block_gates(qkv_r_ref, gact_r_ref, cst, "r", group)
    _gdn_chains_prep(chunks_f + chunks_r, cst)
    states = [sf_ref[...], sr_ref[...]]
    for step in range(group):
        gf = step
        gr = group - 1 - step
        states, (out_f, out_r) = _gdn_scan_step(states, [chunks_f[gf], chunks_r[gr]], cst)
        of_ref[gf * c:(gf + 1) * c, :] = out_f
        or_ref[gr * c:(gr + 1) * c, :] = out_r
    sf_ref[...] = states[0]
    sr_ref[...] = states[1]


def _gdn_scan(qkv, gact, group=GDN_GROUP):
    b, s, w3 = qkv.shape
    group = min(group, s // GDN_CHUNK)
    r = group * GDN_CHUNK
    nblk = s // r
    consts = _gdn_consts(group)
    const_arrays = [jnp.asarray(consts[n], BF16 if n in _GDN_BF16_CONSTS else F32) for n in _GDN_CONST_ORDER]
    fwd = lambda bi, ci: (bi, ci, 0)
    rev = lambda bi, ci: (bi, nblk - 1 - ci, 0)
    in_specs = [pl.BlockSpec((None, r, w3), fwd), pl.BlockSpec((None, r, LANES), fwd),
                pl.BlockSpec((None, r, w3), rev), pl.BlockSpec((None, r, LANES), rev)]
    in_specs += [pl.BlockSpec(a.shape, lambda bi, ci: (0, 0)) for a in const_arrays]
    return pl.pallas_call(
        functools.partial(_gdn_scan_kernel, group=group),
        grid=(b, nblk),
        in_specs=in_specs,
        out_specs=[pl.BlockSpec((None, r, A_W), fwd), pl.BlockSpec((None, r, A_W), rev)],
        out_shape=[jax.ShapeDtypeStruct((b, s, A_W), F32)] * 2,
        scratch_shapes=[pltpu.VMEM((A_W, A_W), F32)] * 2,
        compiler_params=_cparams("parallel", "arbitrary"),
        name="gdn_scan",
    )(qkv, gact, qkv, gact, *const_arrays)


MIN_DENOM = 2.0 ** -80
SHIFT_LANES = 3


def _augment_kv(k_ref, v_ref, kaug_ref, vaug_ref):
    s = k_ref.shape[0]
    lane = lax.broadcasted_iota(jnp.int32, (s, LANES), 1)
    k = k_ref[...]
    kaug_ref[:, 0:LANES] = k
    kaug_ref[:, LANES:] = jnp.where(lane < SHIFT_LANES, 1.0, 0.0).astype(BF16)
    vaug_ref[:, 0:LANES] = v_ref[...]
    vaug_ref[:, LANES:] = jnp.ones((s, LANES), BF16)
    kk = k.astype(F32)
    kk = kk * kk
    norms = []
    for half in range(2):
        n2 = jnp.sum(jnp.where((lane // HEAD_DIM) == half, kk, 0.0), axis=-1, keepdims=True)
        norms.append(jnp.max(n2, axis=0, keepdims=True))
    return norms


def _stack_halves(q):
    lane = lax.broadcasted_iota(jnp.int32, q.shape, 1)
    zero = jnp.zeros_like(q)
    return jnp.concatenate([jnp.where(lane < HEAD_DIM, q, zero), jnp.where(lane >= HEAD_DIM, q, zero)], axis=0)


def _score_bound(qs, knorms):
    rows = qs.shape[0]
    qf = qs.astype(F32)
    q2 = jnp.sum(qf * qf, axis=-1, keepdims=True)
    row = lax.broadcasted_iota(jnp.int32, (rows, 1), 0)
    k2 = jnp.where(row < rows // 2, knorms[0], knorms[1])
    return jnp.sqrt(q2 * k2) * 1.001 + 1e-30


def _augment_q(qs, shift):
    rows = qs.shape[0]
    hi, mid, lo = (t.astype(F32) for t in _split3(-shift))
    lane = lax.broadcasted_iota(jnp.int32, (rows, LANES), 1)
    m_cols = jnp.where(lane == 0, hi, jnp.where(lane == 1, mid, jnp.where(lane == 2, lo, 0.0)))
    return jnp.concatenate([qs, m_cols.astype(BF16)], axis=1)


def _softmax_pv(q_aug, kaug_ref, vaug_ref, key0, nchunk, kc, align, unroll, weight_fn=None, clamp=False):
    rows = q_aug.shape[0]

    def scores(j):
        start = pl.multiple_of(key0 + j * kc, align)
        return _dot_nt(q_aug, kaug_ref[pl.ds(start, kc), :])

    def weighted_values(j, sc):
        start = pl.multiple_of(key0 + j * kc, align)
        p = jnp.exp2(jnp.minimum(sc, 0.0) if clamp else sc)
        if weight_fn is not None:
            p = p * weight_fn(j)
        return _dot(p.astype(BF16), vaug_ref[pl.ds(start, kc), :])

    if unroll is None:
        acc = None
        sc = scores(0)
        for j in range(nchunk):
            sc_next = scores(j + 1) if j + 1 < nchunk else None
            part = weighted_values(j, sc)
            acc = part if acc is None else acc + part
            sc = sc_next
        return acc

    def body(j, acc):
        return acc + weighted_values(j, scores(j))

    return lax.fori_loop(0, nchunk, body, jnp.zeros((rows, 2 * LANES), F32), unroll=unroll)


def _row_max(qs, k_ref, key0, nchunk, kc, align, weight_fn=None):
    def body(j, m_lane):
        start = pl.multiple_of(key0 + j * kc, align)
        sc = _dot_nt(qs, k_ref[pl.ds(start, kc), :])
        if weight_fn is not None:
            sc = jnp.where(weight_fn(j) > 0.0, sc, NEG_BIG)
        for c0 in range(0, kc, LANES):
            m_lane = jnp.maximum(m_lane, sc[:, c0:c0 + LANES])
        return m_lane

    m_lane = lax.fori_loop(0, nchunk, body, jnp.full((qs.shape[0], LANES), NEG_BIG, F32))
    return jnp.max(m_lane, axis=-1, keepdims=True)


def _attention_blocks(nblk, per_iter, block_fn, finish_fn, knorms, k_ref, kaug_ref, vaug_ref):
    per_iter = math.gcd(nblk, per_iter)

    def fast(ii, lmin):
        blocks = []
        for t in range(per_iter):
            i = ii * per_iter + t
            qs, key0, nchunk, kc, align, weight_fn = block_fn(i)
            blocks.append((i, _augment_q(qs, _score_bound(qs, knorms)), key0, nchunk, kc, align, weight_fn))
        for i, q_aug, key0, nchunk, kc, align, weight_fn in blocks:
            acc = _softmax_pv(q_aug, kaug_ref, vaug_ref, key0, nchunk, kc, align, None, weight_fn)
            finish_fn(i, acc)
            lmin = jnp.minimum(lmin, jnp.min(acc[:, LANES:].reshape(-1, 8, LANES), axis=0))
        return lmin

    lmin = lax.fori_loop(0, nblk // per_iter, fast, jnp.full((8, LANES), 3e38, F32))
    all_ok = jnp.min(lmin) >= MIN_DENOM

    @pl.when(jnp.logical_not(all_ok))
    def _():
        def exact(i, carry):
            qs, key0, nchunk, kc, align, weight_fn = block_fn(i)
            m = _row_max(qs, k_ref, key0, nchunk, kc, align, weight_fn)
            finish_fn(i, _softmax_pv(_augment_q(qs, m), kaug_ref, vaug_ref, key0, nchunk, kc, align, 1, weight_fn,
                                     clamp=weight_fn is not None))
            return carry

        lax.fori_loop(0, nblk, exact, 0)


def _dilated_counts(tq):
    r = DIL_REACH
    ii = np.arange(tq)[:, None]
    m = np.arange(tq + 4 * r)[None, :]
    d = 2 * r + ii - m
    count = np.zeros(d.shape, np.float32)
    for window, dil in DIL_PATTERNS:
        count += ((d % dil) == 0) & (np.abs(d) <= window // 2)
    return jnp.asarray(count.reshape(tq, -1, tq).transpose(1, 0, 2))


def _dilated_kernel(q_ref, k_ref, v_ref, cnt_ref, o_ref, kaug_ref, vaug_ref, *, tq, bpc):
    s = k_ref.shape[0]
    nwin = 1 + 2 * DIL_REACH // tq
    knorms = _augment_kv(k_ref, v_ref, kaug_ref, vaug_ref)

    def block(i):
        q0 = pl.multiple_of(i * tq, tq)
        ws = jnp.clip(q0 - DIL_REACH, 0, s - nwin * tq)
        mb0 = (2 * DIL_REACH - (q0 - ws)) // tq

        def weights(j):
            cnt = jnp.concatenate([cnt_ref[mb0 + j * bpc + t] for t in range(bpc)], axis=1)
            return jnp.concatenate([cnt, cnt], axis=0)

        qs = _stack_halves(q_ref[pl.ds(q0, tq), :])
        return qs, ws, nwin // bpc, bpc * tq, tq, weights

    def finish(i, acc):
        lane = lax.broadcasted_iota(jnp.int32, (tq, LANES), 1)
        out = jnp.where(lane < HEAD_DIM, acc[0:tq, 0:LANES] / acc[0:tq, LANES:],
                        acc[tq:, 0:LANES] / acc[tq:, LANES:])
        o_ref[pl.ds(pl.multiple_of(i * tq, tq), tq), :] = out.astype(o_ref.dtype)

    _attention_blocks(s // tq, DIL_BLOCKS_PER_ITER, block, finish, knorms, k_ref, kaug_ref, vaug_ref)


def _dilated(bqkv):
    b, s, _ = bqkv.shape
    tq = DIL_TQ
    npair = B_W // LANES
    counts = _dilated_counts(tq)
    nwin = 1 + 2 * DIL_REACH // tq
    slab = lambda off: pl.BlockSpec((None, s, LANES), lambda bi, p: (bi, 0, off + p))
    return pl.pallas_call(
        functools.partial(_dilated_kernel, tq=tq, bpc=DIL_BPC if nwin % DIL_BPC == 0 else 1),
        grid=(b, npair),
        in_specs=[slab(0), slab(npair), slab(2 * npair), pl.BlockSpec(counts.shape, lambda bi, p: (0, 0, 0))],
        out_specs=slab(0),
        out_shape=jax.ShapeDtypeStruct((b, s, B_W), BF16),
        scratch_shapes=[pltpu.VMEM((s, 2 * LANES), BF16)] * 2,
        compiler_params=_cparams("parallel", "parallel"),
        name="dilated_attn",
    )(bqkv, bqkv, bqkv, counts)


def _diff_kernel(lam_ref, q_ref, k_ref, v_ref, sw_ref, o_ref, kaug_ref, vaug_ref, *, lambda_init, tq, kc):
    s = k_ref.shape[0]
    knorms = _augment_kv(k_ref, v_ref, kaug_ref, vaug_ref)
    lp = lam_ref[...]
    lam = (jnp.exp(jnp.sum(lp[0:1] * lp[1:2], axis=-1, keepdims=True))
           - jnp.exp(jnp.sum(lp[2:3] * lp[3:4], axis=-1, keepdims=True)) + lambda_init)
    sw = sw_ref[...]

    def block(i):
        qs = _stack_halves(q_ref[pl.ds(pl.multiple_of(i * tq, tq), tq), :])
        return qs, 0, s // kc, kc, kc, None

    def finish(i, acc):
        o = acc[0:tq, 0:LANES] / acc[0:tq, LANES:] - lam * (acc[tq:, 0:LANES] / acc[tq:, LANES:])
        o = o * lax.rsqrt(jnp.mean(o * o, axis=-1, keepdims=True) + NORM_EPS) * sw
        o_ref[pl.ds(pl.multiple_of(i * tq, tq), tq), :] = (o * (1.0 - lambda_init)).astype(o_ref.dtype)

    _attention_blocks(s // tq, DIFF_BLOCKS_PER_ITER, block, finish, knorms, k_ref, kaug_ref, vaug_ref)


def _diff_attn(cq, ck, cv, lam_params, subln_w, lambda_init):
    b, s, _ = cq.shape
    tq = min(DIFF_TQ, s)
    kc = min(DIFF_KC, s)
    slab = pl.BlockSpec((None, s, LANES), lambda bi, h: (bi, 0, h))
    return pl.pallas_call(
        functools.partial(_diff_kernel, lambda_init=lambda_init, tq=tq, kc=kc),
        grid=(b, DIFF_HEADS),
        in_specs=[pl.BlockSpec((8, LANES), lambda bi, h: (0, 0)), slab, slab, slab,
                  pl.BlockSpec((1, LANES), lambda bi, h: (0, 0))],
        out_specs=slab,
        out_shape=jax.ShapeDtypeStruct((b, s, C_W), BF16),
        scratch_shapes=[pltpu.VMEM((s, 2 * LANES), BF16)] * 2,
        compiler_params=_cparams("parallel", "parallel"),
        name="diff_attn",
    )(lam_params, cq, ck, cv, subln_w.reshape(1, LANES).astype(F32))


def _outproj_kernel(x_ref, of_ref, or_ref, z_ref, ob_ref, oc_ref, gw_ref, bd_ref, w_ref, y_ref):
    o = of_ref[...] + or_ref[...]
    ms = _dot_exact_rhs(o * o, bd_ref[...]) * (1.0 / HEAD_DIM)
    oa = o * lax.rsqrt(ms + NORM_EPS) * gw_ref[...] * _silu(z_ref[...])
    acc = x_ref[...] + _dot(oa.astype(BF16), w_ref[0:A_W, :])
    acc += _dot(ob_ref[...], w_ref[A_W:A_W + B_W, :])
    acc += _dot(oc_ref[...], w_ref[A_W + B_W:, :])
    y_ref[...] = acc


def _outproj(x2d, o_f, o_r, z, o_b, o_c, gdn_norm_w, w_out, tm=512):
    t, d = x2d.shape
    tm = min(tm, t)
    rows = lambda width: pl.BlockSpec((tm, width), lambda i: (i, 0))
    const = lambda shape: pl.BlockSpec(shape, lambda i: (0, 0))
    head = np.arange(A_W) // HEAD_DIM
    bd = jnp.asarray(head[:, None] == head[None, :], BF16)
    gw = jnp.tile(gdn_norm_w.astype(F32), GDN_HEADS)[None, :]
    return pl.pallas_call(
        _outproj_kernel,
        grid=(t // tm,),
        in_specs=[rows(d), rows(A_W), rows(A_W), rows(A_W), rows(B_W), rows(C_W),
                  const((1, A_W)), const((A_W, A_W)), const(w_out.shape)],
        out_specs=rows(d),
        out_shape=jax.ShapeDtypeStruct((t, d), F32),
        compiler_params=_cparams("parallel"),
        name="outproj",
    )(x2d, o_f, o_r, z, o_b, o_c, gw, bd, w_out)


def _ffn_kernel(x_ref, nw_ref, wg_ref, wu_ref, wd_ref, fw_ref, y_ref, *, fc, final_norm):
    x = x_ref[...]
    h = (x * lax.rsqrt(jnp.mean(x * x, axis=-1, keepdims=True) + NORM_EPS) * nw_ref[...]).astype(BF16)
    acc = x
    for c0 in range(0, wg_ref.shape[1], fc):
        g = _dot(h, wg_ref[:, c0:c0 + fc])
        u = _dot(h, wu_ref[:, c0:c0 + fc])
        acc = acc + _dot((_silu(g) * u).astype(BF16), wd_ref[c0:c0 + fc, :])
    if final_norm:
        acc = acc * lax.rsqrt(jnp.mean(acc * acc, axis=-1, keepdims=True) + NORM_EPS) * fw_ref[...]
    y_ref[...] = acc


def _ffn(x2d, norm_w, wg, wu, wd, final_w, final_norm, tm=512, fc=256):
    t, d = x2d.shape
    tm = min(tm, t)
    f = wg.shape[1]
    rows = pl.BlockSpec((tm, d), lambda i: (i, 0))
    const = lambda shape: pl.BlockSpec(shape, lambda i: (0, 0), pipeline_mode=pl.Buffered(1))
    return pl.pallas_call(
        functools.partial(_ffn_kernel, fc=fc, final_norm=final_norm),
        grid=(t // tm,),
        in_specs=[rows, const((1, d)), const((d, f)), const((d, f)), const((f, d)), const((1, d))],
        out_specs=rows,
        out_shape=jax.ShapeDtypeStruct((t, d), F32),
        compiler_params=_cparams("parallel"),
        name="ffn",
    )(x2d, norm_w.reshape(1, d).astype(F32), wg, wu, wd, final_w.reshape(1, d).astype(F32))


def _pad_in_weight(w_in):
    gate_end = 4 * A_W + GATE_W
    d = w_in.shape[0]
    return jnp.concatenate([w_in[:, :gate_end], jnp.zeros((d, LANES - GATE_W), w_in.dtype),
                            w_in[:, gate_end:]], axis=1).astype(BF16)


def _layer(x2d, b, s, cos, sin, attn_norm_w, w_in, conv_w, a_log, dt_bias, gdn_norm_w,
           lq1, lk1, lq2, lk2, subln_w, w_out, ffn_norm_w, w_gate, w_up, w_down,
           final_w, lambda_init, last):
    aqkv, az, gates, bqkv, cq, ck, cv = _norm_inproj(x2d, attn_norm_w.astype(F32), _pad_in_weight(w_in), cos, sin)
    qkv, gact = _gdn_prep(aqkv.reshape(b, s, -1), gates.reshape(b, s, -1), conv_w, a_log, dt_bias)
    o_f, o_r = _gdn_scan(qkv, gact)
    o_b = _dilated(bqkv.reshape(b, s, -1))
    lam_params = jnp.zeros((8, LANES), F32).at[0:4, :HEAD_DIM].set(
        jnp.stack([lq1, lk1, lq2, lk2]).astype(F32))
    o_c = _diff_attn(cq.reshape(b, s, -1), ck.reshape(b, s, -1), cv.reshape(b, s, -1),
                     lam_params, subln_w, lambda_init)
    t = b * s
    x2d = _outproj(x2d, o_f.reshape(t, -1), o_r.reshape(t, -1), az, o_b.reshape(t, -1),
                   o_c.reshape(t, -1), gdn_norm_w, w_out.astype(BF16))
    return _ffn(x2d, ffn_norm_w, w_gate.astype(BF16), w_up.astype(BF16), w_down.astype(BF16),
                final_w, last)


def kernel(x, positions, attn_norm_w, w_in, conv_w, a_log, dt_bias, gdn_norm_w, lambda_q1, lambda_k1,
           lambda_q2, lambda_k2, subln_w, w_out, ffn_norm_w, w_gate, w_up, w_down, final_norm_w):
    b, s, d = x.shape
    depth = w_in.shape[0]
    cos, sin = _rope_tables(positions)
    x2d = x.reshape(b * s, d)
    for l in range(depth):
        lambda_init = 0.8 - 0.6 * math.exp(-0.3 * l)
        x2d = _layer(x2d, b, s, cos, sin, attn_norm_w[l], w_in[l], conv_w[l], a_log[l], dt_bias[l],
                     gdn_norm_w[l], lambda_q1[l], lambda_k1[l], lambda_q2[l], lambda_k2[l], subln_w[l],
                     w_out[l], ffn_norm_w[l], w_gate[l], w_up[l], w_down[l], final_norm_w,
                     lambda_init, l == depth - 1)
    return x2d.reshape(b, s, d)
```

```python
import functools
import math

import numpy as np
import jax
import jax.numpy as jnp
from jax import lax
from jax.experimental import pallas as pl
from jax.experimental.pallas import tpu as pltpu

F32 = jnp.float32
BF16 = jnp.bfloat16

NORM_EPS = 1e-6
ROPE_THETA = 10000.0
HEAD_DIM = 64
LANES = 128

GDN_HEADS = 4
GDN_CHUNK = 64
GDN_GROUP = 4
CONV_K = 5
CONV_HALO = 8

DIL_HEADS = 4
DIL_PATTERNS = ((128, 1), (512, 4), (2048, 16))
DIL_REACH = max(w // 2 for w, _ in DIL_PATTERNS)
DIL_TQ = 256
DIL_BLOCKS_PER_ITER = 8
DIL_BPC = 1

DIFF_HEADS = 4
DIFF_TQ = 256
DIFF_KC = 512
DIFF_BLOCKS_PER_ITER = 8

A_W = GDN_HEADS * HEAD_DIM
B_W = DIL_HEADS * HEAD_DIM
C_W = DIFF_HEADS * 2 * HEAD_DIM
GATE_W = 2 * 2 * GDN_HEADS
NEG_BIG = -1e30
LOG2E = math.log2(math.e)

VMEM_LIMIT = 56 * 1024 * 1024


def _cparams(*sem):
    return pltpu.CompilerParams(dimension_semantics=sem, vmem_limit_bytes=VMEM_LIMIT)


def _split3(x):
    hi = x.astype(BF16)
    r1 = x - hi.astype(F32)
    mid = r1.astype(BF16)
    lo = (r1 - mid.astype(F32)).astype(BF16)
    return hi, mid, lo


def _dot(a, b):
    return jnp.dot(a, b, preferred_element_type=F32)


def _dot_nt(a, b):
    return lax.dot_general(a, b, (((1,), (1,)), ((), ())), preferred_element_type=F32)


def _dot_tn(a, b):
    return lax.dot_general(a, b, (((0,), (0,)), ((), ())), preferred_element_type=F32)


def _dot_exact_rhs(x, m_bf16):
    hi, mid, lo = _split3(x)
    return _dot(hi, m_bf16) + _dot(mid, m_bf16) + _dot(lo, m_bf16)


def _dot_exact_lhs(m_bf16, x):
    hi, mid, lo = _split3(x)
    return _dot(m_bf16, hi) + _dot(m_bf16, mid) + _dot(m_bf16, lo)


def _sigmoid(x):
    return 1.0 / (1.0 + jnp.exp(-x))


def _silu(x):
    return x * _sigmoid(x)


def _rope_kernel(pos_ref, inv_ref, sign_ref, cos_ref, sin_ref):
    ang = pos_ref[...].astype(F32) * inv_ref[...]
    cos_ref[...] = jnp.cos(ang)
    sin_ref[...] = jnp.sin(ang) * sign_ref[...]


def _rope_tables(positions):
    t = positions.size
    tr = min(t, 1024)
    half = HEAD_DIM // 2
    inv = ROPE_THETA ** (-jnp.arange(0, HEAD_DIM, 2, dtype=F32) / HEAD_DIM)
    inv_row = jnp.tile(inv, LANES // half)[None, :]
    sign_row = jnp.asarray(np.where((np.arange(LANES) % HEAD_DIM) < half, -1.0, 1.0), F32)[None, :]
    row = pl.BlockSpec((1, LANES), lambda i: (0, 0))
    out = pl.BlockSpec((tr, LANES), lambda i: (i, 0))
    return pl.pallas_call(
        _rope_kernel,
        grid=(t // tr,),
        in_specs=[pl.BlockSpec((tr, 1), lambda i: (i, 0)), row, row],
        out_specs=[out, out],
        out_shape=[jax.ShapeDtypeStruct((t, LANES), F32)] * 2,
        compiler_params=_cparams("parallel"),
        name="rope_tables",
    )(positions.reshape(t, 1), inv_row, sign_row)


def _rope(y, cos, sin):
    half = HEAD_DIM // 2
    lane = lax.broadcasted_iota(jnp.int32, cos.shape, 1)
    first_half = (lane % HEAD_DIM) < half
    slabs = []
    for c0 in range(0, y.shape[1], LANES):
        ys = y[:, c0:c0 + LANES]
        partner = jnp.where(first_half, pltpu.roll(ys, LANES - half, 1), pltpu.roll(ys, half, 1))
        slabs.append(ys * cos + partner * sin)
    return jnp.concatenate(slabs, axis=1)


def _inproj_kernel(x_ref, nw_ref, w_ref, cos_ref, sin_ref,
                   aqkv_ref, az_ref, gate_ref, bqkv_ref, cq_ref, ck_ref, cv_ref):
    x = x_ref[...]
    h = x * lax.rsqrt(jnp.mean(x * x, axis=-1, keepdims=True) + NORM_EPS) * nw_ref[...]
    h = h.astype(BF16)
    cos = cos_ref[...]
    sin = sin_ref[...]
    scale = HEAD_DIM ** -0.5 * LOG2E

    def proj(start, width):
        return _dot(h, w_ref[:, start:start + width])

    o = 0
    aqkv_ref[...] = proj(o, 3 * A_W)
    o += 3 * A_W
    az_ref[...] = proj(o, A_W)
    o += A_W
    gate_ref[...] = proj(o, LANES)
    o += LANES
    bqkv_ref[:, 0:B_W] = (_rope(proj(o, B_W), cos, sin) * scale).astype(BF16)
    o += B_W
    bqkv_ref[:, B_W:2 * B_W] = _rope(proj(o, B_W), cos, sin).astype(BF16)
    o += B_W
    bqkv_ref[:, 2 * B_W:3 * B_W] = proj(o, B_W).astype(BF16)
    o += B_W
    cq_ref[...] = (_rope(proj(o, C_W), cos, sin) * scale).astype(BF16)
    o += C_W
    ck_ref[...] = _rope(proj(o, C_W), cos, sin).astype(BF16)
    o += C_W
    cv_ref[...] = proj(o, C_W).astype(BF16)


def _norm_inproj(x2d, norm_w, w_pad, cos, sin, tm=512):
    t, d = x2d.shape
    tm = min(tm, t)
    np_ = w_pad.shape[1]
    rows = lambda width: pl.BlockSpec((tm, width), lambda i: (i, 0))
    out_w = (3 * A_W, A_W, LANES, 3 * B_W, C_W, C_W, C_W)
    out_dt = (F32, F32, F32, BF16, BF16, BF16, BF16)
    return pl.pallas_call(
        _inproj_kernel,
        grid=(t // tm,),
        in_specs=[rows(d), pl.BlockSpec((1, d), lambda i: (0, 0)),
                  pl.BlockSpec((d, np_), lambda i: (0, 0)), rows(LANES), rows(LANES)],
        out_specs=[rows(w) for w in out_w],
        out_shape=[jax.ShapeDtypeStruct((t, w), dt) for w, dt in zip(out_w, out_dt)],
        compiler_params=_cparams("parallel"),
        name="norm_inproj",
    )(x2d, norm_w.reshape(1, d), w_pad, cos, sin)


def _gdn_prep_kernel(prev_ref, cur_ref, next_ref, gate_ref, cw_ref, gp_ref, bd_ref,
                     qkv_ref, gact_ref, ext_ref):
    i = pl.program_id(1)
    n = pl.num_programs(1)
    tr = cur_ref.shape[0]
    ext_ref[0:CONV_HALO, :] = jnp.where(i > 0, prev_ref[...], 0.0)
    ext_ref[CONV_HALO:CONV_HALO + tr, :] = cur_ref[...]
    ext_ref[CONV_HALO + tr:, :] = jnp.where(i < n - 1, next_ref[...], 0.0)
    pad = (CONV_K - 1) // 2
    acc = None
    for j in range(CONV_K):
        term = ext_ref[pl.ds(CONV_HALO - pad + j, tr), :] * cw_ref[j:j + 1, :]
        acc = term if acc is None else acc + term
    y = _silu(acc)
    bd = bd_ref[...]
    dk_scale = HEAD_DIM ** -0.5
    for part, mul in ((0, dk_scale), (1, 1.0)):
        t = y[:, part * A_W:(part + 1) * A_W]
        ss = _dot_exact_rhs(t * t, bd)
        qkv_ref[:, part * A_W:(part + 1) * A_W] = t * (lax.rsqrt(ss + 1e-6) * mul)
    qkv_ref[:, 2 * A_W:] = y[:, 2 * A_W:]
    a = gate_ref[...]
    z = a + gp_ref[1:2, :]
    softplus = jnp.maximum(z, 0.0) + jnp.log(1.0 + jnp.exp(-jnp.abs(z)))
    g = gp_ref[0:1, :] * softplus
    lane = lax.broadcasted_iota(jnp.int32, a.shape, 1)
    gact_ref[...] = jnp.where(lane < GATE_W // 2, g, jnp.where(lane < GATE_W, _sigmoid(a), 0.0))


def _gdn_prep(aqkv, gates, conv_w, a_log, dt_bias, tr=512):
    b, s, w = aqkv.shape
    tr = min(tr, s)
    hb = tr // CONV_HALO
    nblk8 = s // CONV_HALO
    cw = jnp.zeros((8, w), F32).at[:CONV_K].set(conv_w.astype(F32))
    gp = jnp.zeros((8, LANES), F32)
    gp = gp.at[0, :GATE_W // 2].set(-jnp.exp(a_log.astype(F32).reshape(-1)))
    gp = gp.at[1, :GATE_W // 2].set(dt_bias.astype(F32).reshape(-1))
    head = np.arange(A_W) // HEAD_DIM
    bd = jnp.asarray(head[:, None] == head[None, :], BF16)
    const = lambda shape: pl.BlockSpec(shape, lambda bi, i: (0, 0))
    return pl.pallas_call(
        _gdn_prep_kernel,
        grid=(b, s // tr),
        in_specs=[
            pl.BlockSpec((None, CONV_HALO, w), lambda bi, i: (bi, jnp.maximum(i * hb - 1, 0), 0)),
            pl.BlockSpec((None, tr, w), lambda bi, i: (bi, i, 0)),
            pl.BlockSpec((None, CONV_HALO, w), lambda bi, i: (bi, jnp.minimum((i + 1) * hb, nblk8 - 1), 0)),
            pl.BlockSpec((None, tr, LANES), lambda bi, i: (bi, i, 0)),
            const((8, w)), const((8, LANES)), const((A_W, A_W)),
        ],
        out_specs=[pl.BlockSpec((None, tr, w), lambda bi, i: (bi, i, 0)),
                   pl.BlockSpec((None, tr, LANES), lambda bi, i: (bi, i, 0))],
        out_shape=[jax.ShapeDtypeStruct((b, s, w), F32), jax.ShapeDtypeStruct((b, s, LANES), F32)],
        scratch_shapes=[pltpu.VMEM((tr + 2 * CONV_HALO, w), F32)],
        compiler_params=_cparams("parallel", "parallel"),
        name="gdn_prep",
    )(aqkv, aqkv, aqkv, gates, cw, gp, bd)


def _gdn_consts(group):
    c = GDN_CHUNK
    w = A_W
    r = group * c
    i = np.arange(c)[:, None]
    j = np.arange(w)[None, :] % c
    col_head = np.arange(w)[None, :] // c
    row = np.arange(w)[:, None]
    t = np.arange(r)
    same_chunk = (t[:, None] // c) == (t[None, :] // c)
    tot = np.arange(16)[:, None] == (t[None, :] // c)
    consts = {}
    for name, rev in (("f", False), ("r", True)):
        ge = (i <= j) if rev else (i >= j)
        consts["tril_" + name] = ge.astype(np.float32)
        consts["strict_" + name] = (ge & (i != j)).astype(np.float32)
        tri = (t[None, :] >= t[:, None]) if rev else (t[None, :] <= t[:, None])
        consts["cum_" + name] = np.concatenate([tri & same_chunk, tot], axis=0).astype(np.float32)
        upper = (i >= j) if rev else (i <= j)
        consts["upper_" + name] = np.tile(upper, (group, 1)).astype(np.float32)
    blk16 = (i // 16) == (j // 16)
    blk32 = (i // 32) == (j // 32)
    consts["m16"] = blk16.astype(np.float32)
    consts["m32"] = (blk32 & ~blk16).astype(np.float32)
    consts["m64"] = (~blk32).astype(np.float32)
    consts["eye"] = (i == j).astype(np.float32)
    consts["bd"] = ((row // c) == col_head).astype(np.float32)
    consts["bd16"] = consts["bd"]
    for d, name in enumerate(("f", "r")):
        sel = np.zeros((LANES, 2 * w), np.float32)
        for part in range(2):
            for h in range(GDN_HEADS):
                lane = part * 2 * GDN_HEADS + d * GDN_HEADS + h
                sel[lane, part * w + h * c: part * w + (h + 1) * c] = 1.0
        consts["sel_" + name] = sel
    return consts


_GDN_CONST_ORDER = ("tril_f", "strict_f", "cum_f", "upper_f", "sel_f", "tril_r", "strict_r", "cum_r", "upper_r",
                    "sel_r", "m16", "m32", "m64", "eye", "bd", "bd16")
_GDN_BF16_CONSTS = ("cum_f", "cum_r", "sel_f", "sel_r", "bd16")


def _expand(x16, bd16):
    c = GDN_CHUNK
    zero = jnp.zeros((c, LANES), BF16)
    blocks = []
    for h in range(GDN_HEADS):
        tile = (h * c) // LANES
        kept = x16[:, tile * LANES:(tile + 1) * LANES] * bd16[h * c:(h + 1) * c, tile * LANES:(tile + 1) * LANES]
        blocks.append(jnp.concatenate([kept if t == tile else zero for t in range(A_W // LANES)], axis=1))
    return jnp.concatenate(blocks, axis=0)


def _gdn_chains_prep(chains, cst):
    c = GDN_CHUNK
    w = A_W
    bd16 = cst["bd16"]
    eye = cst["eye"]
    for ch in chains:
        kq = _dot_nt(jnp.concatenate([ch["kb"], ch["q"]], axis=0).astype(BF16), _expand(ch["k"].astype(BF16), bd16))
        ch["low"] = kq[0:c] * ch["decay"] * cst["strict_" + ch["name"]]
        ch["intra"] = (kq[c:2 * c] * ch["decay"] * cst["tril_" + ch["name"]]).astype(BF16)
    for ch in chains:
        n1 = -(ch["low"] * cst["m16"])
        ch["p"] = eye + n1
        ch["nb"] = n1.astype(BF16)
    for ch in chains:
        ch["nb"] = _dot(ch["nb"], _expand(ch["nb"], bd16)).astype(BF16)
    for _ in range(2):
        for ch in chains:
            r = _dot(jnp.concatenate([ch["nb"], ch["p"].astype(BF16)], axis=0), _expand(ch["nb"], bd16))
            ch["nb"] = r[0:c].astype(BF16)
            ch["p"] = ch["p"] + r[c:2 * c]
    for ch in chains:
        ch["inv"] = ch["p"] + _dot(ch["p"].astype(BF16), _expand(ch["nb"], bd16))
    for mname in ("m32", "m64"):
        for ch in chains:
            ch["invb"] = ch["inv"].astype(BF16)
            ch["t1"] = _dot((ch["low"] * cst[mname]).astype(BF16), _expand(ch["invb"], bd16)).astype(BF16)
        for ch in chains:
            ch["inv"] = ch["inv"] - _dot(ch["invb"], _expand(ch["t1"], bd16))
    for ch in chains:
        rhs = jnp.concatenate([_expand(ch["vb"].astype(BF16), bd16), _expand(ch["kbg"].astype(BF16), bd16)], axis=1)
        uk = _dot(ch["inv"].astype(BF16), rhs)
        ch["u"] = uk[:, 0:w]
        ch["kq_lhs"] = jnp.concatenate([uk[:, w:2 * w], ch["qg"]], axis=0).astype(BF16)


def _gdn_scan_step(states, chs, cst):
    c = GDN_CHUNK
    ks = [_dot(ch["kq_lhs"], st.astype(BF16)) for st, ch in zip(states, chs)]
    v16 = [(ch["u"] - k_[0:c]).astype(BF16) for k_, ch in zip(ks, chs)]
    upd = [_dot_tn(ch["kd"], v) for v, ch in zip(v16, chs)]
    outs = [k_[c:2 * c] + _dot(ch["intra"], _expand(v, cst["bd16"])) for k_, v, ch in zip(ks, v16, chs)]
    states = [st * ch["egl"] + u_ * cst["bd"] for st, u_, ch in zip(states, upd, chs)]
    return states, outs


def _gdn_block_gates(qkv_ref, gact_ref, cst, name, group):
    c = GDN_CHUNK
    w = A_W
    r = group * c
    qkv = qkv_ref[...]
    q = qkv[:, 0:w]
    k = qkv[:, w:2 * w]
    v = qkv[:, 2 * w:3 * w]
    gsel = _dot_exact_rhs(gact_ref[...], cst["sel_" + name])
    gb = gsel[:, 0:w]
    bb = gsel[:, w:2 * w]
    cum = _dot_exact_lhs(cst["cum_" + name], gb)
    gc = cum[0:r]
    gl_rows = cum[r:]
    gr_rows = _dot_exact_lhs(cst["cum_" + name][r:], gb * cst["upper_" + name])
    kb = k * bb
    vb = v * bb
    eg = jnp.exp(gc)
    kbg = kb * eg
    qg = q * eg
    chunks = []
    for g in range(group):
        rows = slice(g * c, (g + 1) * c)
        gl = gl_rows[g:g + 1]
        chunks.append(dict(name=name, q=q[rows], k=k[rows], kb=kb[rows], vb=vb[rows], kbg=kbg[rows],
                           decay=jnp.exp(jnp.minimum(gc[rows] - gr_rows[g:g + 1], 0.0)), qg=qg[rows],
                           kd=(k[rows] * jnp.exp(gl - gc[rows])).astype(BF16), egl=jnp.exp(gl)))
    return chunks


def _gdn_scan_kernel(*refs, group):
    nconst = len(_GDN_CONST_ORDER)
    qkv_f_ref, gact_f_ref, qkv_r_ref, gact_r_ref = refs[0:4]
    cst = {n: r[...] for n, r in zip(_GDN_CONST_ORDER, refs[4:4 + nconst])}
    of_ref, or_ref, sf_ref, sr_ref = refs[4 + nconst:]

    @pl.when(pl.program_id(1) == 0)
    def _():
        sf_ref[...] = jnp.zeros_like(sf_ref)
        sr_ref[...] = jnp.zeros_like(sr_ref)

    c = GDN_CHUNK
    chunks_f = _gdn_block_gates(qkv_f_ref, gact_f_ref, cst, "f", group)
    chunks_r = _gdn_block_gates(qkv_r_ref, gact_r_ref, cst, "r", group)
    _gdn_chains_prep(chunks_f + chunks_r, cst)
    states = [sf_ref[...], sr_ref[...]]
    for step in range(group):
        gf = step
        gr = group - 1 - step
        states, (out_f, out_r) = _gdn_scan_step(states, [chunks_f[gf], chunks_r[gr]], cst)
        of_ref[gf * c:(gf + 1) * c, :] = out_f
        or_ref[gr * c:(gr + 1) * c, :] = out_r
    sf_ref[...] = states[0]
    sr_ref[...] = states[1]


def _gdn_scan(qkv, gact, group=GDN_GROUP):
    b, s, w3 = qkv.shape
    group = min(group, s // GDN_CHUNK)
    r = group * GDN_CHUNK
    nblk = s // r
    consts = _gdn_consts(group)
    const_arrays = [jnp.asarray(consts[n], BF16 if n in _GDN_BF16_CONSTS else F32) for n in _GDN_CONST_ORDER]
    fwd = lambda bi, ci: (bi, ci, 0)
    rev = lambda bi, ci: (bi, nblk - 1 - ci, 0)
    in_specs = [pl.BlockSpec((None, r, w3), fwd), pl.BlockSpec((None, r, LANES), fwd),
                pl.BlockSpec((None, r, w3), rev), pl.BlockSpec((None, r, LANES), rev)]
    in_specs += [pl.BlockSpec(a.shape, lambda bi, ci: (0, 0)) for a in const_arrays]
    return pl.pallas_call(
        functools.partial(_gdn_scan_kernel, group=group),
        grid=(b, nblk),
        in_specs=in_specs,
        out_specs=[pl.BlockSpec((None, r, A_W), fwd), pl.BlockSpec((None, r, A_W), rev)],
        out_shape=[jax.ShapeDtypeStruct((b, s, A_W), F32)] * 2,
        scratch_shapes=[pltpu.VMEM((A_W, A_W), F32)] * 2,
        compiler_params=_cparams("parallel", "arbitrary"),
        name="gdn_scan",
    )(qkv, gact, qkv, gact, *const_arrays)


MIN_DENOM = 2.0 ** -80
SHIFT_LANES = 3


def _augment_kv(k_ref, v_ref, kaug_ref, vaug_ref):
    s = k_ref.shape[0]
    lane = lax.broadcasted_iota(jnp.int32, (s, LANES), 1)
    k = k_ref[...]
    kaug_ref[:, 0:LANES] = k
    kaug_ref[:, LANES:] = jnp.where(lane < SHIFT_LANES, 1.0, 0.0).astype(BF16)
    vaug_ref[:, 0:LANES] = v_ref[...]
    vaug_ref[:, LANES:] = jnp.ones((s, LANES), BF16)
    kk = k.astype(F32)
    kk = kk * kk
    norms = []
    for half in range(2):
        n2 = jnp.sum(jnp.where((lane // HEAD_DIM) == half, kk, 0.0), axis=-1, keepdims=True)
        norms.append(jnp.max(n2, axis=0, keepdims=True))
    return norms


def _stack_halves(q):
    lane = lax.broadcasted_iota(jnp.int32, q.shape, 1)
    zero = jnp.zeros_like(q)
    return jnp.concatenate([jnp.where(lane < HEAD_DIM, q, zero), jnp.where(lane >= HEAD_DIM, q, zero)], axis=0)


def _score_bound(qs, knorms):
    rows = qs.shape[0]
    qf = qs.astype(F32)
    q2 = jnp.sum(qf * qf, axis=-1, keepdims=True)
    row = lax.broadcasted_iota(jnp.int32, (rows, 1), 0)
    k2 = jnp.where(row < rows // 2, knorms[0], knorms[1])
    return jnp.sqrt(q2 * k2) * 1.001 + 1e-30


def _augment_q(qs, shift):
    rows = qs.shape[0]
    hi, mid, lo = (t.astype(F32) for t in _split3(-shift))
    lane = lax.broadcasted_iota(jnp.int32, (rows, LANES), 1)
    m_cols = jnp.where(lane == 0, hi, jnp.where(lane == 1, mid, jnp.where(lane == 2, lo, 0.0)))
    return jnp.concatenate([qs, m_cols.astype(BF16)], axis=1)


def _softmax_pv(q_aug, kaug_ref, vaug_ref, key0, nchunk, kc, align, unroll, weight_fn=None, clamp=False):
    rows = q_aug.shape[0]

    def scores(j):
        start = pl.multiple_of(key0 + j * kc, align)
        return _dot_nt(q_aug, kaug_ref[pl.ds(start, kc), :])

    def weighted_values(j, sc):
        start = pl.multiple_of(key0 + j * kc, align)
        p = jnp.exp2(jnp.minimum(sc, 0.0) if clamp else sc)
        if weight_fn is not None:
            p = p * weight_fn(j)
        return _dot(p.astype(BF16), vaug_ref[pl.ds(start, kc), :])

    if unroll is None:
        acc = None
        sc = scores(0)
        for j in range(nchunk):
            sc_next = scores(j + 1) if j + 1 < nchunk else None
            part = weighted_values(j, sc)
            acc = part if acc is None else acc + part
            sc = sc_next
        return acc

    def body(j, acc):
        return acc + weighted_values(j, scores(j))

    return lax.fori_loop(0, nchunk, body, jnp.zeros((rows, 2 * LANES), F32), unroll=unroll)


def _row_max(qs, k_ref, key0, nchunk, kc, align, weight_fn=None):
    def body(j, m_lane):
        start = pl.multiple_of(key0 + j * kc, align)
        sc = _dot_nt(qs, k_ref[pl.ds(start, kc), :])
        if weight_fn is not None:
            sc = jnp.where(weight_fn(j) > 0.0, sc, NEG_BIG)
        for c0 in range(0, kc, LANES):
            m_lane = jnp.maximum(m_lane, sc[:, c0:c0 + LANES])
        return m_lane

    m_lane = lax.fori_loop(0, nchunk, body, jnp.full((qs.shape[0], LANES), NEG_BIG, F32))
    return jnp.max(m_lane, axis=-1, keepdims=True)


def _attention_blocks(nblk, per_iter, block_fn, finish_fn, knorms, k_ref, kaug_ref, vaug_ref):
    per_iter = math.gcd(nblk, per_iter)

    def fast(ii, lmin):
        blocks = []
        for t in range(per_iter):
            i = ii * per_iter + t
            qs, key0, nchunk, kc, align, weight_fn = block_fn(i)
            blocks.append((i, _augment_q(qs, _score_bound(qs, knorms)), key0, nchunk, kc, align, weight_fn))
        for i, q_aug, key0, nchunk, kc, align, weight_fn in blocks:
            acc = _softmax_pv(q_aug, kaug_ref, vaug_ref, key0, nchunk, kc, align, None, weight_fn)
            finish_fn(i, acc)
            lmin = jnp.minimum(lmin, jnp.min(acc[:, LANES:].reshape(-1, 8, LANES), axis=0))
        return lmin

    lmin = lax.fori_loop(0, nblk // per_iter, fast, jnp.full((8, LANES), 3e38, F32))
    all_ok = jnp.min(lmin) >= MIN_DENOM

    @pl.when(jnp.logical_not(all_ok))
    def _():
        def exact(i, carry):
            qs, key0, nchunk, kc, align, weight_fn = block_fn(i)
            m = _row_max(qs, k_ref, key0, nchunk, kc, align, weight_fn)
            finish_fn(i, _softmax_pv(_augment_q(qs, m), kaug_ref, vaug_ref, key0, nchunk, kc, align, 1, weight_fn,
                                     clamp=weight_fn is not None))
            return carry

        lax.fori_loop(0, nblk, exact, 0)


def _dilated_counts(tq):
    r = DIL_REACH
    ii = np.arange(tq)[:, None]
    m = np.arange(tq + 4 * r)[None, :]
    d = 2 * r + ii - m
    count = np.zeros(d.shape, np.float32)
    for window, dil in DIL_PATTERNS:
        count += ((d % dil) == 0) & (np.abs(d) <= window // 2)
    return jnp.asarray(count.reshape(tq, -1, tq).transpose(1, 0, 2))


def _dilated_kernel(q_ref, k_ref, v_ref, cnt_ref, o_ref, kaug_ref, vaug_ref, *, tq, bpc):
    s = k_ref.shape[0]
    nwin = 1 + 2 * DIL_REACH // tq
    knorms = _augment_kv(k_ref, v_ref, kaug_ref, vaug_ref)

    def block(i):
        q0 = pl.multiple_of(i * tq, tq)
        ws = jnp.clip(q0 - DIL_REACH, 0, s - nwin * tq)
        mb0 = (2 * DIL_REACH - (q0 - ws)) // tq

        def weights(j):
            cnt = jnp.concatenate([cnt_ref[mb0 + j * bpc + t] for t in range(bpc)], axis=1)
            return jnp.concatenate([cnt, cnt], axis=0)

        qs = _stack_halves(q_ref[pl.ds(q0, tq), :])
        return qs, ws, nwin // bpc, bpc * tq, tq, weights

    def finish(i, acc):
        lane = lax.broadcasted_iota(jnp.int32, (tq, LANES), 1)
        out = jnp.where(lane < HEAD_DIM, acc[0:tq, 0:LANES] / acc[0:tq, LANES:],
                        acc[tq:, 0:LANES] / acc[tq:, LANES:])
        o_ref[pl.ds(pl.multiple_of(i * tq, tq), tq), :] = out.astype(o_ref.dtype)

    _attention_blocks(s // tq, DIL_BLOCKS_PER_ITER, block, finish, knorms, k_ref, kaug_ref, vaug_ref)


def _dilated(bqkv):
    b, s, _ = bqkv.shape
    tq = DIL_TQ
    npair = B_W // LANES
    counts = _dilated_counts(tq)
    nwin = 1 + 2 * DIL_REACH // tq
    slab = lambda off: pl.BlockSpec((None, s, LANES), lambda bi, p: (bi, 0, off + p))
    return pl.pallas_call(
        functools.partial(_dilated_kernel, tq=tq, bpc=DIL_BPC if nwin % DIL_BPC == 0 else 1),
        grid=(b, npair),
        in_specs=[slab(0), slab(npair), slab(2 * npair), pl.BlockSpec(counts.shape, lambda bi, p: (0, 0, 0))],
        out_specs=slab(0),
        out_shape=jax.ShapeDtypeStruct((b, s, B_W), BF16),
        scratch_shapes=[pltpu.VMEM((s, 2 * LANES), BF16)] * 2,
        compiler_params=_cparams("parallel", "parallel"),
        name="dilated_attn",
    )(bqkv, bqkv, bqkv, counts)


def _diff_kernel(lam_ref, q_ref, k_ref, v_ref, sw_ref, o_ref, kaug_ref, vaug_ref, *, lambda_init, tq, kc):
    s = k_ref.shape[0]
    knorms = _augment_kv(k_ref, v_ref, kaug_ref, vaug_ref)
    lp = lam_ref[...]
    lam = (jnp.exp(jnp.sum(lp[0:1] * lp[1:2], axis=-1, keepdims=True))
           - jnp.exp(jnp.sum(lp[2:3] * lp[3:4], axis=-1, keepdims=True)) + lambda_init)
    sw = sw_ref[...]

    def block(i):
        qs = _stack_halves(q_ref[pl.ds(pl.multiple_of(i * tq, tq), tq), :])
        return qs, 0, s // kc, kc, kc, None

    def finish(i, acc):
        o = acc[0:tq, 0:LANES] / acc[0:tq, LANES:] - lam * (acc[tq:, 0:LANES] / acc[tq:, LANES:])
        o = o * lax.rsqrt(jnp.mean(o * o, axis=-1, keepdims=True) + NORM_EPS) * sw
        o_ref[pl.ds(pl.multiple_of(i * tq, tq), tq), :] = (o * (1.0 - lambda_init)).astype(o_ref.dtype)

    _attention_blocks(s // tq, DIFF_BLOCKS_PER_ITER, block, finish, knorms, k_ref, kaug_ref, vaug_ref)


def _diff_attn(cq, ck, cv, lam_params, subln_w, lambda_init):
    b, s, _ = cq.shape
    tq = min(DIFF_TQ, s)
    kc = min(DIFF_KC, s)
    slab = pl.BlockSpec((None, s, LANES), lambda bi, h: (bi, 0, h))
    return pl.pallas_call(
        functools.partial(_diff_kernel, lambda_init=lambda_init, tq=tq, kc=kc),
        grid=(b, DIFF_HEADS),
        in_specs=[pl.BlockSpec((8, LANES), lambda bi, h: (0, 0)), slab, slab, slab,
                  pl.BlockSpec((1, LANES), lambda bi, h: (0, 0))],
        out_specs=slab,
        out_shape=jax.ShapeDtypeStruct((b, s, C_W), BF16),
        scratch_shapes=[pltpu.VMEM((s, 2 * LANES), BF16)] * 2,
        compiler_params=_cparams("parallel", "parallel"),
        name="diff_attn",
    )(lam_params, cq, ck, cv, subln_w.reshape(1, LANES).astype(F32))


def _outproj_kernel(x_ref, of_ref, or_ref, z_ref, ob_ref, oc_ref, gw_ref, bd_ref, w_ref, y_ref):
    o = of_ref[...] + or_ref[...]
    ms = _dot_exact_rhs(o * o, bd_ref[...]) * (1.0 / HEAD_DIM)
    oa = o * lax.rsqrt(ms + NORM_EPS) * gw_ref[...] * _silu(z_ref[...])
    acc = x_ref[...] + _dot(oa.astype(BF16), w_ref[0:A_W, :])
    acc += _dot(ob_ref[...], w_ref[A_W:A_W + B_W, :])
    acc += _dot(oc_ref[...], w_ref[A_W + B_W:, :])
    y_ref[...] = acc


def _outproj(x2d, o_f, o_r, z, o_b, o_c, gdn_norm_w, w_out, tm=512):
    t, d = x2d.shape
    tm = min(tm, t)
    rows = lambda width: pl.BlockSpec((tm, width), lambda i: (i, 0))
    const = lambda shape: pl.BlockSpec(shape, lambda i: (0, 0))
    head = np.arange(A_W) // HEAD_DIM
    bd = jnp.asarray(head[:, None] == head[None, :], BF16)
    gw = jnp.tile(gdn_norm_w.astype(F32), GDN_HEADS)[None, :]
    return pl.pallas_call(
        _outproj_kernel,
        grid=(t // tm,),
        in_specs=[rows(d), rows(A_W), rows(A_W), rows(A_W), rows(B_W), rows(C_W),
                  const((1, A_W)), const((A_W, A_W)), const(w_out.shape)],
        out_specs=rows(d),
        out_shape=jax.ShapeDtypeStruct((t, d), F32),
        compiler_params=_cparams("parallel"),
        name="outproj",
    )(x2d, o_f, o_r, z, o_b, o_c, gw, bd, w_out)


def _ffn_kernel(x_ref, nw_ref, wg_ref, wu_ref, wd_ref, fw_ref, y_ref, *, fc, final_norm):
    x = x_ref[...]
    h = (x * lax.rsqrt(jnp.mean(x * x, axis=-1, keepdims=True) + NORM_EPS) * nw_ref[...]).astype(BF16)
    acc = x
    for c0 in range(0, wg_ref.shape[1], fc):
        g = _dot(h, wg_ref[:, c0:c0 + fc])
        u = _dot(h, wu_ref[:, c0:c0 + fc])
        acc = acc + _dot((_silu(g) * u).astype(BF16), wd_ref[c0:c0 + fc, :])
    if final_norm:
        acc = acc * lax.rsqrt(jnp.mean(acc * acc, axis=-1, keepdims=True) + NORM_EPS) * fw_ref[...]
    y_ref[...] = acc


def _ffn(x2d, norm_w, wg, wu, wd, final_w, final_norm, tm=512, fc=256):
    t, d = x2d.shape
    tm = min(tm, t)
    f = wg.shape[1]
    rows = pl.BlockSpec((tm, d), lambda i: (i, 0))
    const = lambda shape: pl.BlockSpec(shape, lambda i: (0, 0), pipeline_mode=pl.Buffered(1))
    return pl.pallas_call(
        functools.partial(_ffn_kernel, fc=fc, final_norm=final_norm),
        grid=(t // tm,),
        in_specs=[rows, const((1, d)), const((d, f)), const((d, f)), const((f, d)), const((1, d))],
        out_specs=rows,
        out_shape=jax.ShapeDtypeStruct((t, d), F32),
        compiler_params=_cparams("parallel"),
        name="ffn",
    )(x2d, norm_w.reshape(1, d).astype(F32), wg, wu, wd, final_w.reshape(1, d).astype(F32))


def _pad_in_weight(w_in):
    gate_end = 4 * A_W + GATE_W
    d = w_in.shape[0]
    return jnp.concatenate([w_in[:, :gate_end], jnp.zeros((d, LANES - GATE_W), w_in.dtype),
                            w_in[:, gate_end:]], axis=1).astype(BF16)


def _layer(x2d, b, s, cos, sin, attn_norm_w, w_in, conv_w, a_log, dt_bias, gdn_norm_w,
           lq1, lk1, lq2, lk2, subln_w, w_out, ffn_norm_w, w_gate, w_up, w_down,
           final_w, lambda_init, last):
    aqkv, az, gates, bqkv, cq, ck, cv = _norm_inproj(x2d, attn_norm_w.astype(F32), _pad_in_weight(w_in), cos, sin)
    qkv, gact = _gdn_prep(aqkv.reshape(b, s, -1), gates.reshape(b, s, -1), conv_w, a_log, dt_bias)
    o_f, o_r = _gdn_scan(qkv, gact)
    o_b = _dilated(bqkv.reshape(b, s, -1))
    lam_params = jnp.zeros((8, LANES), F32).at[0:4, :HEAD_DIM].set(
        jnp.stack([lq1, lk1, lq2, lk2]).astype(F32))
    o_c = _diff_attn(cq.reshape(b, s, -1), ck.reshape(b, s, -1), cv.reshape(b, s, -1),
                     lam_params, subln_w, lambda_init)
    t = b * s
    x2d = _outproj(x2d, o_f.reshape(t, -1), o_r.reshape(t, -1), az, o_b.reshape(t, -1),
                   o_c.reshape(t, -1), gdn_norm_w, w_out.astype(BF16))
    return _ffn(x2d, ffn_norm_w, w_gate.astype(BF16), w_up.astype(BF16), w_down.astype(BF16),
                final_w, last)


def kernel(x, positions, attn_norm_w, w_in, conv_w, a_log, dt_bias, gdn_norm_w, lambda_q1, lambda_k1,
           lambda_q2, lambda_k2, subln_w, w_out, ffn_norm_w, w_gate, w_up, w_down, final_norm_w):
    b, s, d = x.shape
    depth = w_in.shape[0]
    cos, sin = _rope_tables(positions)
    x2d = x.reshape(b * s, d)
    for l in range(depth):
        lambda_init = 0.8 - 0.6 * math.exp(-0.3 * l)
        x2d = _layer(x2d, b, s, cos, sin, attn_norm_w[l], w_in[l], conv_w[l], a_log[l], dt_bias[l],
                     gdn_norm_w[l], lambda_q1[l], lambda_k1[l], lambda_q2[l], lambda_k2[l], subln_w[l],
                     w_out[l], ffn_norm_w[l], w_gate[l], w_up[l], w_down[l], final_norm_w,
                     lambda_init, l == depth - 1)
    return x2d.reshape(b, s, d)
```

```python
import functools
import math

import numpy as np
import jax
import jax.numpy as jnp
from jax import lax
from jax.experimental import pallas as pl
from jax.experimental.pallas import tpu as pltpu

F32 = jnp.float32
BF16 = jnp.bfloat16

NORM_EPS = 1e-6
ROPE_THETA = 10000.0
HEAD_DIM = 64
LANES = 128

GDN_HEADS = 4
GDN_CHUNK = 64
GDN_GROUP = 8
CONV_K = 5
CONV_HALO = 8

DIL_HEADS = 4
DIL_PATTERNS = ((128, 1), (512, 4), (2048, 16))
DIL_REACH = max(w // 2 for w, _ in DIL_PATTERNS)
DIL_TQ = 256
DIL_BLOCKS_PER_ITER = 8
DIL_BPC = 1

DIFF_HEADS = 4
DIFF_TQ = 256
DIFF_KC = 512
DIFF_BLOCKS_PER_ITER = 8

A_W = GDN_HEADS * HEAD_DIM
B_W = DIL_HEADS * HEAD_DIM
C_W = DIFF_HEADS * 2 * HEAD_DIM
GATE_W = 2 * 2 * GDN_HEADS
NEG_BIG = -1e30
LOG2E = math.log2(math.e)

VMEM_LIMIT = 56 * 1024 * 1024


def _cparams(*sem):
    return pltpu.CompilerParams(dimension_semantics=sem, vmem_limit_bytes=VMEM_LIMIT)


def _split3(x):
    hi = x.astype(BF16)
    r1 = x - hi.astype(F32)
    mid = r1.astype(BF16)
    lo = (r1 - mid.astype(F32)).astype(BF16)
    return hi, mid, lo


def _dot(a, b):
    return jnp.dot(a, b, preferred_element_type=F32)


def _dot_nt(a, b):
    return lax.dot_general(a, b, (((1,), (1,)), ((), ())), preferred_element_type=F32)


def _dot_tn(a, b):
    return lax.dot_general(a, b, (((0,), (0,)), ((), ())), preferred_element_type=F32)


def _dot_exact_rhs(x, m_bf16):
    hi, mid, lo = _split3(x)
    return _dot(hi, m_bf16) + _dot(mid, m_bf16) + _dot(lo, m_bf16)


def _dot_exact_lhs(m_bf16, x):
    hi, mid, lo = _split3(x)
    return _dot(m_bf16, hi) + _dot(m_bf16, mid) + _dot(m_bf16, lo)


def _sigmoid(x):
    return 1.0 / (1.0 + jnp.exp(-x))


def _silu(x):
    return x * _sigmoid(x)


def _rope_kernel(pos_ref, inv_ref, sign_ref, cos_ref, sin_ref):
    ang = pos_ref[...].astype(F32) * inv_ref[...]
    cos_ref[...] = jnp.cos(ang)
    sin_ref[...] = jnp.sin(ang) * sign_ref[...]


def _rope_tables(positions):
    t = positions.size
    tr = min(t, 1024)
    half = HEAD_DIM // 2
    inv = ROPE_THETA ** (-jnp.arange(0, HEAD_DIM, 2, dtype=F32) / HEAD_DIM)
    inv_row = jnp.tile(inv, LANES // half)[None, :]
    sign_row = jnp.asarray(np.where((np.arange(LANES) % HEAD_DIM) < half, -1.0, 1.0), F32)[None, :]
    row = pl.BlockSpec((1, LANES), lambda i: (0, 0))
    out = pl.BlockSpec((tr, LANES), lambda i: (i, 0))
    return pl.pallas_call(
        _rope_kernel,
        grid=(t // tr,),
        in_specs=[pl.BlockSpec((tr, 1), lambda i: (i, 0)), row, row],
        out_specs=[out, out],
        out_shape=[jax.ShapeDtypeStruct((t, LANES), F32)] * 2,
        compiler_params=_cparams("parallel"),
        name="rope_tables",
    )(positions.reshape(t, 1), inv_row, sign_row)


def _rope(y, cos, sin):
    half = HEAD_DIM // 2
    lane = lax.broadcasted_iota(jnp.int32, cos.shape, 1)
    first_half = (lane % HEAD_DIM) < half
    slabs = []
    for c0 in range(0, y.shape[1], LANES):
        ys = y[:, c0:c0 + LANES]
        partner = jnp.where(first_half, pltpu.roll(ys, LANES - half, 1), pltpu.roll(ys, half, 1))
        slabs.append(ys * cos + partner * sin)
    return jnp.concatenate(slabs, axis=1)


def _inproj_kernel(x_ref, nw_ref, w_ref, cos_ref, sin_ref,
                   aqkv_ref, az_ref, gate_ref, bqkv_ref, cq_ref, ck_ref, cv_ref):
    x = x_ref[...]
    h = x * lax.rsqrt(jnp.mean(x * x, axis=-1, keepdims=True) + NORM_EPS) * nw_ref[...]
    h = h.astype(BF16)
    cos = cos_ref[...]
    sin = sin_ref[...]
    scale = HEAD_DIM ** -0.5 * LOG2E

    def proj(start, width):
        return _dot(h, w_ref[:, start:start + width])

    o = 0
    aqkv_ref[...] = proj(o, 3 * A_W)
    o += 3 * A_W
    az_ref[...] = proj(o, A_W)
    o += A_W
    gate_ref[...] = proj(o, LANES)
    o += LANES
    bqkv_ref[:, 0:B_W] = (_rope(proj(o, B_W), cos, sin) * scale).astype(BF16)
    o += B_W
    bqkv_ref[:, B_W:2 * B_W] = _rope(proj(o, B_W), cos, sin).astype(BF16)
    o += B_W
    bqkv_ref[:, 2 * B_W:3 * B_W] = proj(o, B_W).astype(BF16)
    o += B_W
    cq_ref[...] = (_rope(proj(o, C_W), cos, sin) * scale).astype(BF16)
    o += C_W
    ck_ref[...] = _rope(proj(o, C_W), cos, sin).astype(BF16)
    o += C_W
    cv_ref[...] = proj(o, C_W).astype(BF16)


def _norm_inproj(x2d, norm_w, w_pad, cos, sin, tm=512):
    t, d = x2d.shape
    tm = min(tm, t)
    np_ = w_pad.shape[1]
    rows = lambda width: pl.BlockSpec((tm, width), lambda i: (i, 0))
    out_w = (3 * A_W, A_W, LANES, 3 * B_W, C_W, C_W, C_W)
    out_dt = (F32, F32, F32, BF16, BF16, BF16, BF16)
    return pl.pallas_call(
        _inproj_kernel,
        grid=(t // tm,),
        in_specs=[rows(d), pl.BlockSpec((1, d), lambda i: (0, 0)),
                  pl.BlockSpec((d, np_), lambda i: (0, 0)), rows(LANES), rows(LANES)],
        out_specs=[rows(w) for w in out_w],
        out_shape=[jax.ShapeDtypeStruct((t, w), dt) for w, dt in zip(out_w, out_dt)],
        compiler_params=_cparams("parallel"),
        name="norm_inproj",
    )(x2d, norm_w.reshape(1, d), w_pad, cos, sin)


def _gdn_prep_kernel(prev_ref, cur_ref, next_ref, gate_ref, cw_ref, gp_ref, bd_ref,
                     qkv_ref, gact_ref, ext_ref):
    i = pl.program_id(1)
    n = pl.num_programs(1)
    tr = cur_ref.shape[0]
    ext_ref[0:CONV_HALO, :] = jnp.where(i > 0, prev_ref[...], 0.0)
    ext_ref[CONV_HALO:CONV_HALO + tr, :] = cur_ref[...]
    ext_ref[CONV_HALO + tr:, :] = jnp.where(i < n - 1, next_ref[...], 0.0)
    pad = (CONV_K - 1) // 2
    acc = None
    for j in range(CONV_K):
        term = ext_ref[pl.ds(CONV_HALO - pad + j, tr), :] * cw_ref[j:j + 1, :]
        acc = term if acc is None else acc + term
    y = _silu(acc)
    bd = bd_ref[...]
    dk_scale = HEAD_DIM ** -0.5
    for part, mul in ((0, dk_scale), (1, 1.0)):
        t = y[:, part * A_W:(part + 1) * A_W]
        ss = _dot_exact_rhs(t * t, bd)
        qkv_ref[:, part * A_W:(part + 1) * A_W] = t * (lax.rsqrt(ss + 1e-6) * mul)
    qkv_ref[:, 2 * A_W:] = y[:, 2 * A_W:]
    a = gate_ref[...]
    z = a + gp_ref[1:2, :]
    softplus = jnp.maximum(z, 0.0) + jnp.log(1.0 + jnp.exp(-jnp.abs(z)))
    g = gp_ref[0:1, :] * softplus
    lane = lax.broadcasted_iota(jnp.int32, a.shape, 1)
    gact_ref[...] = jnp.where(lane < GATE_W // 2, g, jnp.where(lane < GATE_W, _sigmoid(a), 0.0))


def _gdn_prep(aqkv, gates, conv_w, a_log, dt_bias, tr=512):
    b, s, w = aqkv.shape
    tr = min(tr, s)
    hb = tr // CONV_HALO
    nblk8 = s // CONV_HALO
    cw = jnp.zeros((8, w), F32).at[:CONV_K].set(conv_w.astype(F32))
    gp = jnp.zeros((8, LANES), F32)
    gp = gp.at[0, :GATE_W // 2].set(-jnp.exp(a_log.astype(F32).reshape(-1)))
    gp = gp.at[1, :GATE_W // 2].set(dt_bias.astype(F32).reshape(-1))
    head = np.arange(A_W) // HEAD_DIM
    bd = jnp.asarray(head[:, None] == head[None, :], BF16)
    const = lambda shape: pl.BlockSpec(shape, lambda bi, i: (0, 0))
    return pl.pallas_call(
        _gdn_prep_kernel,
        grid=(b, s // tr),
        in_specs=[
            pl.BlockSpec((None, CONV_HALO, w), lambda bi, i: (bi, jnp.maximum(i * hb - 1, 0), 0)),
            pl.BlockSpec((None, tr, w), lambda bi, i: (bi, i, 0)),
            pl.BlockSpec((None, CONV_HALO, w), lambda bi, i: (bi, jnp.minimum((i + 1) * hb, nblk8 - 1), 0)),
            pl.BlockSpec((None, tr, LANES), lambda bi, i: (bi, i, 0)),
            const((8, w)), const((8, LANES)), const((A_W, A_W)),
        ],
        out_specs=[pl.BlockSpec((None, tr, w), lambda bi, i: (bi, i, 0)),
                   pl.BlockSpec((None, tr, LANES), lambda bi, i: (bi, i, 0))],
        out_shape=[jax.ShapeDtypeStruct((b, s, w), F32), jax.ShapeDtypeStruct((b, s, LANES), F32)],
        scratch_shapes=[pltpu.VMEM((tr + 2 * CONV_HALO, w), F32)],
        compiler_params=_cparams("parallel", "parallel"),
        name="gdn_prep",
    )(aqkv, aqkv, aqkv, gates, cw, gp, bd)


def _gdn_consts(group):
    c = GDN_CHUNK
    w = A_W
    r = group * c
    i = np.arange(c)[:, None]
    j = np.arange(w)[None, :] % c
    col_head = np.arange(w)[None, :] // c
    row = np.arange(w)[:, None]
    t = np.arange(r)
    same_chunk = (t[:, None] // c) == (t[None, :] // c)
    tot = np.arange(16)[:, None] == (t[None, :] // c)
    consts = {}
    for name, rev in (("f", False), ("r", True)):
        ge = (i <= j) if rev else (i >= j)
        consts["tril_" + name] = ge.astype(np.float32)
        consts["strict_" + name] = (ge & (i != j)).astype(np.float32)
        tri = (t[None, :] >= t[:, None]) if rev else (t[None, :] <= t[:, None])
        consts["cum_" + name] = np.concatenate([tri & same_chunk, tot], axis=0).astype(np.float32)
        upper = (i >= j) if rev else (i <= j)
        consts["upper_" + name] = np.tile(upper, (group, 1)).astype(np.float32)
    blk16 = (i // 16) == (j // 16)
    blk32 = (i // 32) == (j // 32)
    consts["m16"] = blk16.astype(np.float32)
    consts["m32"] = (blk32 & ~blk16).astype(np.float32)
    consts["m64"] = (~blk32).astype(np.float32)
    consts["eye"] = (i == j).astype(np.float32)
    consts["bd"] = ((row // c) == col_head).astype(np.float32)
    consts["bd16"] = consts["bd"]
    for d, name in enumerate(("f", "r")):
        sel = np.zeros((LANES, 2 * w), np.float32)
        for part in range(2):
            for h in range(GDN_HEADS):
                lane = part * 2 * GDN_HEADS + d * GDN_HEADS + h
                sel[lane, part * w + h * c: part * w + (h + 1) * c] = 1.0
        consts["sel_" + name] = sel
    return consts


_GDN_CONST_ORDER = ("tril_f", "strict_f", "cum_f", "upper_f", "sel_f", "tril_r", "strict_r", "cum_r", "upper_r",
                    "sel_r", "m16", "m32", "m64", "eye", "bd", "bd16")
_GDN_BF16_CONSTS = ("cum_f", "cum_r", "sel_f", "sel_r", "bd16")


def _expand(x16, bd16):
    c = GDN_CHUNK
    zero = jnp.zeros((c, LANES), BF16)
    blocks = []
    for h in range(GDN_HEADS):
        tile = (h * c) // LANES
        kept = x16[:, tile * LANES:(tile + 1) * LANES] * bd16[h * c:(h + 1) * c, tile * LANES:(tile + 1) * LANES]
        blocks.append(jnp.concatenate([kept if t == tile else zero for t in range(A_W // LANES)], axis=1))
    return jnp.concatenate(blocks, axis=0)


def _gdn_chains_prep(chains, cst):
    c = GDN_CHUNK
    w = A_W
    bd16 = cst["bd16"]
    eye = cst["eye"]
    for ch in chains:
        kq = _dot_nt(jnp.concatenate([ch["kb"], ch["q"]], axis=0).astype(BF16), _expand(ch["k"].astype(BF16), bd16))
        ch["low"] = kq[0:c] * ch["decay"] * cst["strict_" + ch["name"]]
        ch["intra"] = (kq[c:2 * c] * ch["decay"] * cst["tril_" + ch["name"]]).astype(BF16)
    for ch in chains:
        n1 = -(ch["low"] * cst["m16"])
        ch["p"] = eye + n1
        ch["nb"] = n1.astype(BF16)
    for ch in chains:
        ch["nb"] = _dot(ch["nb"], _expand(ch["nb"], bd16)).astype(BF16)
    for _ in range(2):
        for ch in chains:
            r = _dot(jnp.concatenate([ch["nb"], ch["p"].astype(BF16)], axis=0), _expand(ch["nb"], bd16))
            ch["nb"] = r[0:c].astype(BF16)
            ch["p"] = ch["p"] + r[c:2 * c]
    for ch in chains:
        ch["inv"] = ch["p"] + _dot(ch["p"].astype(BF16), _expand(ch["nb"], bd16))
    for mname in ("m32", "m64"):
        for ch in chains:
            ch["invb"] = ch["inv"].astype(BF16)
            ch["t1"] = _dot((ch["low"] * cst[mname]).astype(BF16), _expand(ch["invb"], bd16)).astype(BF16)
        for ch in chains:
            ch["inv"] = ch["inv"] - _dot(ch["invb"], _expand(ch["t1"], bd16))
    for ch in chains:
        rhs = jnp.concatenate([_expand(ch["vb"].astype(BF16), bd16), _expand(ch["kbg"].astype(BF16), bd16)], axis=1)
        uk = _dot(ch["inv"].astype(BF16), rhs)
        ch["u"] = uk[:, 0:w]
        ch["kq_lhs"] = jnp.concatenate([uk[:, w:2 * w], ch["qg"]], axis=0).astype(BF16)


def _gdn_scan_step(states, chs, cst):
    c = GDN_CHUNK
    ks = [_dot(ch["kq_lhs"], st.astype(BF16)) for st, ch in zip(states, chs)]
    v16 = [(ch["u"] - k_[0:c]).astype(BF16) for k_, ch in zip(ks, chs)]
    upd = [_dot_tn(ch["kd"], v) for v, ch in zip(v16, chs)]
    outs = [k_[c:2 * c] + _dot(ch["intra"], _expand(v, cst["bd16"])) for k_, v, ch in zip(ks, v16, chs)]
    states = [st * ch["egl"] + u_ * cst["bd"] for st, u_, ch in zip(states, upd, chs)]
    return states, outs


def _gdn_block_gates(qkv_ref, gact_ref, cst, name, group):
    c = GDN_CHUNK
    w = A_W
    r = group * c
    qkv = qkv_ref[...]
    q = qkv[:, 0:w]
    k = qkv[:, w:2 * w]
    v = qkv[:, 2 * w:3 * w]
    gsel = _dot_exact_rhs(gact_ref[...], cst["sel_" + name])
    gb = gsel[:, 0:w]
    bb = gsel[:, w:2 * w]
    cum = _dot_exact_lhs(cst["cum_" + name], gb)
    gc = cum[0:r]
    gl_rows = cum[r:]
    gr_rows = _dot_exact_lhs(cst["cum_" + name][r:], gb * cst["upper_" + name])
    kb = k * bb
    vb = v * bb
    eg = jnp.exp(gc)
    kbg = kb * eg
    qg = q * eg
    chunks = []
    for g in range(group):
        rows = slice(g * c, (g + 1) * c)
        gl = gl_rows[g:g + 1]
        chunks.append(dict(name=name, q=q[rows], k=k[rows], kb=kb[rows], vb=vb[rows], kbg=kbg[rows],
                           decay=jnp.exp(jnp.minimum(gc[rows] - gr_rows[g:g + 1], 0.0)), qg=qg[rows],
                           kd=(k[rows] * jnp.exp(gl - gc[rows])).astype(BF16), egl=jnp.exp(gl)))
    return chunks


def _gdn_scan_kernel(*refs, group):
    nconst = len(_GDN_CONST_ORDER)
    qkv_f_ref, gact_f_ref, qkv_r_ref, gact_r_ref = refs[0:4]
    cst = {n: r[...] for n, r in zip(_GDN_CONST_ORDER, refs[4:4 + nconst])}
    of_ref, or_ref, sf_ref, sr_ref = refs[4 + nconst:]

    @pl.when(pl.program_id(1) == 0)
    def _():
        sf_ref[...] = jnp.zeros_like(sf_ref)
        sr_ref[...] = jnp.zeros_like(sr_ref)

    c = GDN_CHUNK
    chunks_f = _gdn_block_gates(qkv_f_ref, gact_f_ref, cst, "f", group)
    chunks_r = _gdn_block_gates(qkv_r_ref, gact_r_ref, cst, "r", group)
    _gdn_chains_prep(chunks_f + chunks_r, cst)
    states = [sf_ref[...], sr_ref[...]]
    for step in range(group):
        gf = step
        gr = group - 1 - step
        states, (out_f, out_r) = _gdn_scan_step(states, [chunks_f[gf], chunks_r[gr]], cst)
        of_ref[gf * c:(gf + 1) * c, :] = out_f
        or_ref[gr * c:(gr + 1) * c, :] = out_r
    sf_ref[...] = states[0]
    sr_ref[...] = states[1]


def _gdn_scan(qkv, gact, group=GDN_GROUP):
    b, s, w3 = qkv.shape
    group = min(group, s // GDN_CHUNK)
    r = group * GDN_CHUNK
    nblk = s // r
    consts = _gdn_consts(group)
    const_arrays = [jnp.asarray(consts[n], BF16 if n in _GDN_BF16_CONSTS else F32) for n in _GDN_CONST_ORDER]
    fwd = lambda bi, ci: (bi, ci, 0)
    rev = lambda bi, ci: (bi, nblk - 1 - ci, 0)
    in_specs = [pl.BlockSpec((None, r, w3), fwd), pl.BlockSpec((None, r, LANES), fwd),
                pl.BlockSpec((None, r, w3), rev), pl.BlockSpec((None, r, LANES), rev)]
    in_specs += [pl.BlockSpec(a.shape, lambda bi, ci: (0, 0)) for a in const_arrays]
    return pl.pallas_call(
        functools.partial(_gdn_scan_kernel, group=group),
        grid=(b, nblk),
        in_specs=in_specs,
        out_specs=[pl.BlockSpec((None, r, A_W), fwd), pl.BlockSpec((None, r, A_W), rev)],
        out_shape=[jax.ShapeDtypeStruct((b, s, A_W), F32)] * 2,
        scratch_shapes=[pltpu.VMEM((A_W, A_W), F32)] * 2,
        compiler_params=_cparams("parallel", "arbitrary"),
        name="gdn_scan",
    )(qkv, gact, qkv, gact, *const_arrays)


MIN_DENOM = 2.0 ** -80
SHIFT_LANES = 3


def _augment_kv(k_ref, v_ref, kaug_ref, vaug_ref):
    s = k_ref.shape[0]
    lane = lax.broadcasted_iota(jnp.int32, (s, LANES), 1)
    k = k_ref[...]
    kaug_ref[:, 0:LANES] = k
    kaug_ref[:, LANES:] = jnp.where(lane < SHIFT_LANES, 1.0, 0.0).astype(BF16)
    vaug_ref[:, 0:LANES] = v_ref[...]
    vaug_ref[:, LANES:] = jnp.ones((s, LANES), BF16)
    kk = k.astype(F32)
    kk = kk * kk
    norms = []
    for half in range(2):
        n2 = jnp.sum(jnp.where((lane // HEAD_DIM) == half, kk, 0.0), axis=-1, keepdims=True)
        norms.append(jnp.max(n2, axis=0, keepdims=True))
    return norms


def _stack_halves(q):
    lane = lax.broadcasted_iota(jnp.int32, q.shape, 1)
    zero = jnp.zeros_like(q)
    return jnp.concatenate([jnp.where(lane < HEAD_DIM, q, zero), jnp.where(lane >= HEAD_DIM, q, zero)], axis=0)


def _score_bound(qs, knorms):
    rows = qs.shape[0]
    qf = qs.astype(F32)
    q2 = jnp.sum(qf * qf, axis=-1, keepdims=True)
    row = lax.broadcasted_iota(jnp.int32, (rows, 1), 0)
    k2 = jnp.where(row < rows // 2, knorms[0], knorms[1])
    return jnp.sqrt(q2 * k2) * 1.001 + 1e-30


def _augment_q(qs, shift):
    rows = qs.shape[0]
    hi, mid, lo = (t.astype(F32) for t in _split3(-shift))
    lane = lax.broadcasted_iota(jnp.int32, (rows, LANES), 1)
    m_cols = jnp.where(lane == 0, hi, jnp.where(lane == 1, mid, jnp.where(lane == 2, lo, 0.0)))
    return jnp.concatenate([qs, m_cols.astype(BF16)], axis=1)


def _softmax_pv(q_aug, kaug_ref, vaug_ref, key0, nchunk, kc, align, unroll, weight_fn=None, clamp=False):
    rows = q_aug.shape[0]

    def scores(j):
        start = pl.multiple_of(key0 + j * kc, align)
        return _dot_nt(q_aug, kaug_ref[pl.ds(start, kc), :])

    def weighted_values(j, sc):
        start = pl.multiple_of(key0 + j * kc, align)
        p = jnp.exp2(jnp.minimum(sc, 0.0) if clamp else sc)
        if weight_fn is not None:
            p = p * weight_fn(j)
        return _dot(p.astype(BF16), vaug_ref[pl.ds(start, kc), :])

    if unroll is None:
        acc = None
        sc = scores(0)
        for j in range(nchunk):
            sc_next = scores(j + 1) if j + 1 < nchunk else None
            part = weighted_values(j, sc)
            acc = part if acc is None else acc + part
            sc = sc_next
        return acc

    def body(j, acc):
        return acc + weighted_values(j, scores(j))

    return lax.fori_loop(0, nchunk, body, jnp.zeros((rows, 2 * LANES), F32), unroll=unroll)


def _row_max(qs, k_ref, key0, nchunk, kc, align, weight_fn=None):
    def body(j, m_lane):
        start = pl.multiple_of(key0 + j * kc, align)
        sc = _dot_nt(qs, k_ref[pl.ds(start, kc), :])
        if weight_fn is not None:
            sc = jnp.where(weight_fn(j) > 0.0, sc, NEG_BIG)
        for c0 in range(0, kc, LANES):
            m_lane = jnp.maximum(m_lane, sc[:, c0:c0 + LANES])
        return m_lane

    m_lane = lax.fori_loop(0, nchunk, body, jnp.full((qs.shape[0], LANES), NEG_BIG, F32))
    return jnp.max(m_lane, axis=-1, keepdims=True)


def _attention_blocks(nblk, per_iter, block_fn, finish_fn, knorms, k_ref, kaug_ref, vaug_ref):
    per_iter = math.gcd(nblk, per_iter)

    def fast(ii, lmin):
        blocks = []
        for t in range(per_iter):
            i = ii * per_iter + t
            qs, key0, nchunk, kc, align, weight_fn = block_fn(i)
            blocks.append((i, _augment_q(qs, _score_bound(qs, knorms)), key0, nchunk, kc, align, weight_fn))
        for i, q_aug, key0, nchunk, kc, align, weight_fn in blocks:
            acc = _softmax_pv(q_aug, kaug_ref, vaug_ref, key0, nchunk, kc, align, None, weight_fn)
            finish_fn(i, acc)
            lmin = jnp.minimum(lmin, jnp.min(acc[:, LANES:].reshape(-1, 8, LANES), axis=0))
        return lmin

    lmin = lax.fori_loop(0, nblk // per_iter, fast, jnp.full((8, LANES), 3e38, F32))
    all_ok = jnp.min(lmin) >= MIN_DENOM

    @pl.when(jnp.logical_not(all_ok))
    def _():
        def exact(i, carry):
            qs, key0, nchunk, kc, align, weight_fn = block_fn(i)
            m = _row_max(qs, k_ref, key0, nchunk, kc, align, weight_fn)
            finish_fn(i, _softmax_pv(_augment_q(qs, m), kaug_ref, vaug_ref, key0, nchunk, kc, align, 1, weight_fn,
                                     clamp=weight_fn is not None))
            return carry

        lax.fori_loop(0, nblk, exact, 0)


def _dilated_counts(tq):
    r = DIL_REACH
    ii = np.arange(tq)[:, None]
    m = np.arange(tq + 4 * r)[None, :]
    d = 2 * r + ii - m
    count = np.zeros(d.shape, np.float32)
    for window, dil in DIL_PATTERNS:
        count += ((d % dil) == 0) & (np.abs(d) <= window // 2)
    return jnp.asarray(count.reshape(tq, -1, tq).transpose(1, 0, 2))


def _dilated_kernel(q_ref, k_ref, v_ref, cnt_ref, o_ref, kaug_ref, vaug_ref, *, tq, bpc):
    s = k_ref.shape[0]
    nwin = 1 + 2 * DIL_REACH // tq
    knorms = _augment_kv(k_ref, v_ref, kaug_ref, vaug_ref)

    def block(i):
        q0 = pl.multiple_of(i * tq, tq)
        ws = jnp.clip(q0 - DIL_REACH, 0, s - nwin * tq)
        mb0 = (2 * DIL_REACH - (q0 - ws)) // tq

        def weights(j):
            cnt = jnp.concatenate([cnt_ref[mb0 + j * bpc + t] for t in range(bpc)], axis=1)
            return jnp.concatenate([cnt, cnt], axis=0)

        qs = _stack_halves(q_ref[pl.ds(q0, tq), :])
        return qs, ws, nwin // bpc, bpc * tq, tq, weights

    def finish(i, acc):
        lane = lax.broadcasted_iota(jnp.int32, (tq, LANES), 1)
        out = jnp.where(lane < HEAD_DIM, acc[0:tq, 0:LANES] / acc[0:tq, LANES:],
                        acc[tq:, 0:LANES] / acc[tq:, LANES:])
        o_ref[pl.ds(pl.multiple_of(i * tq, tq), tq), :] = out.astype(o_ref.dtype)

    _attention_blocks(s // tq, DIL_BLOCKS_PER_ITER, block, finish, knorms, k_ref, kaug_ref, vaug_ref)


def _dilated(bqkv):
    b, s, _ = bqkv.shape
    tq = DIL_TQ
    npair = B_W // LANES
    counts = _dilated_counts(tq)
    nwin = 1 + 2 * DIL_REACH // tq
    slab = lambda off: pl.BlockSpec((None, s, LANES), lambda bi, p: (bi, 0, off + p))
    return pl.pallas_call(
        functools.partial(_dilated_kernel, tq=tq, bpc=DIL_BPC if nwin % DIL_BPC == 0 else 1),
        grid=(b, npair),
        in_specs=[slab(0), slab(npair), slab(2 * npair), pl.BlockSpec(counts.shape, lambda bi, p: (0, 0, 0))],
        out_specs=slab(0),
        out_shape=jax.ShapeDtypeStruct((b, s, B_W), BF16),
        scratch_shapes=[pltpu.VMEM((s, 2 * LANES), BF16)] * 2,
        compiler_params=_cparams("parallel", "parallel"),
        name="dilated_attn",
    )(bqkv, bqkv, bqkv, counts)


def _diff_kernel(lam_ref, q_ref, k_ref, v_ref, sw_ref, o_ref, kaug_ref, vaug_ref, *, lambda_init, tq, kc):
    s = k_ref.shape[0]
    knorms = _augment_kv(k_ref, v_ref, kaug_ref, vaug_ref)
    lp = lam_ref[...]
    lam = (jnp.exp(jnp.sum(lp[0:1] * lp[1:2], axis=-1, keepdims=True))
           - jnp.exp(jnp.sum(lp[2:3] * lp[3:4], axis=-1, keepdims=True)) + lambda_init)
    sw = sw_ref[...]

    def block(i):
        qs = _stack_halves(q_ref[pl.ds(pl.multiple_of(i * tq, tq), tq), :])
        return qs, 0, s // kc, kc, kc, None

    def finish(i, acc):
        o = acc[0:tq, 0:LANES] / acc[0:tq, LANES:] - lam * (acc[tq:, 0:LANES] / acc[tq:, LANES:])
        o = o * lax.rsqrt(jnp.mean(o * o, axis=-1, keepdims=True) + NORM_EPS) * sw
        o_ref[pl.ds(pl.multiple_of(i * tq, tq), tq), :] = (o * (1.0 - lambda_init)).astype(o_ref.dtype)

    _attention_blocks(s // tq, DIFF_BLOCKS_PER_ITER, block, finish, knorms, k_ref, kaug_ref, vaug_ref)


def _diff_attn(cq, ck, cv, lam_params, subln_w, lambda_init):
    b, s, _ = cq.shape
    tq = min(DIFF_TQ, s)
    kc = min(DIFF_KC, s)
    slab = pl.BlockSpec((None, s, LANES), lambda bi, h: (bi, 0, h))
    return pl.pallas_call(
        functools.partial(_diff_kernel, lambda_init=lambda_init, tq=tq, kc=kc),
        grid=(b, DIFF_HEADS),
        in_specs=[pl.BlockSpec((8, LANES), lambda bi, h: (0, 0)), slab, slab, slab,
                  pl.BlockSpec((1, LANES), lambda bi, h: (0, 0))],
        out_specs=slab,
        out_shape=jax.ShapeDtypeStruct((b, s, C_W), BF16),
        scratch_shapes=[pltpu.VMEM((s, 2 * LANES), BF16)] * 2,
        compiler_params=_cparams("parallel", "parallel"),
        name="diff_attn",
    )(lam_params, cq, ck, cv, subln_w.reshape(1, LANES).astype(F32))


def _outproj_kernel(x_ref, of_ref, or_ref, z_ref, ob_ref, oc_ref, gw_ref, bd_ref, w_ref, y_ref):
    o = of_ref[...] + or_ref[...]
    ms = _dot_exact_rhs(o * o, bd_ref[...]) * (1.0 / HEAD_DIM)
    oa = o * lax.rsqrt(ms + NORM_EPS) * gw_ref[...] * _silu(z_ref[...])
    acc = x_ref[...] + _dot(oa.astype(BF16), w_ref[0:A_W, :])
    acc += _dot(ob_ref[...], w_ref[A_W:A_W + B_W, :])
    acc += _dot(oc_ref[...], w_ref[A_W + B_W:, :])
    y_ref[...] = acc


def _outproj(x2d, o_f, o_r, z, o_b, o_c, gdn_norm_w, w_out, tm=512):
    t, d = x2d.shape
    tm = min(tm, t)
    rows = lambda width: pl.BlockSpec((tm, width), lambda i: (i, 0))
    const = lambda shape: pl.BlockSpec(shape, lambda i: (0, 0))
    head = np.arange(A_W) // HEAD_DIM
    bd = jnp.asarray(head[:, None] == head[None, :], BF16)
    gw = jnp.tile(gdn_norm_w.astype(F32), GDN_HEADS)[None, :]
    return pl.pallas_call(
        _outproj_kernel,
        grid=(t // tm,),
        in_specs=[rows(d), rows(A_W), rows(A_W), rows(A_W), rows(B_W), rows(C_W),
                  const((1, A_W)), const((A_W, A_W)), const(w_out.shape)],
        out_specs=rows(d),
        out_shape=jax.ShapeDtypeStruct((t, d), F32),
        compiler_params=_cparams("parallel"),
        name="outproj",
    )(x2d, o_f, o_r, z, o_b, o_c, gw, bd, w_out)


def _ffn_kernel(x_ref, nw_ref, wg_ref, wu_ref, wd_ref, fw_ref, y_ref, *, fc, final_norm):
    x = x_ref[...]
    h = (x * lax.rsqrt(jnp.mean(x * x, axis=-1, keepdims=True) + NORM_EPS) * nw_ref[...]).astype(BF16)
    acc = x
    for c0 in range(0, wg_ref.shape[1], fc):
        g = _dot(h, wg_ref[:, c0:c0 + fc])
        u = _dot(h, wu_ref[:, c0:c0 + fc])
        acc = acc + _dot((_silu(g) * u).astype(BF16), wd_ref[c0:c0 + fc, :])
    if final_norm:
        acc = acc * lax.rsqrt(jnp.mean(acc * acc, axis=-1, keepdims=True) + NORM_EPS) * fw_ref[...]
    y_ref[...] = acc


def _ffn(x2d, norm_w, wg, wu, wd, final_w, final_norm, tm=512, fc=256):
    t, d = x2d.shape
    tm = min(tm, t)
    f = wg.shape[1]
    rows = pl.BlockSpec((tm, d), lambda i: (i, 0))
    const = lambda shape: pl.BlockSpec(shape, lambda i: (0, 0), pipeline_mode=pl.Buffered(1))
    return pl.pallas_call(
        functools.partial(_ffn_kernel, fc=fc, final_norm=final_norm),
        grid=(t // tm,),
        in_specs=[rows, const((1, d)), const((d, f)), const((d, f)), const((f, d)), const((1, d))],
        out_specs=rows,
        out_shape=jax.ShapeDtypeStruct((t, d), F32),
        compiler_params=_cparams("parallel"),
        name="ffn",
    )(x2d, norm_w.reshape(1, d).astype(F32), wg, wu, wd, final_w.reshape(1, d).astype(F32))


def _pad_in_weight(w_in):
    gate_end = 4 * A_W + GATE_W
    d = w_in.shape[0]
    return jnp.concatenate([w_in[:, :gate_end], jnp.zeros((d, LANES - GATE_W), w_in.dtype),
                            w_in[:, gate_end:]], axis=1).astype(BF16)


def _layer(x2d, b, s, cos, sin, attn_norm_w, w_in, conv_w, a_log, dt_bias, gdn_norm_w,
           lq1, lk1, lq2, lk2, subln_w, w_out, ffn_norm_w, w_gate, w_up, w_down,
           final_w, lambda_init, last):
    aqkv, az, gates, bqkv, cq, ck, cv = _norm_inproj(x2d, attn_norm_w.astype(F32), _pad_in_weight(w_in), cos, sin)
    qkv, gact = _gdn_prep(aqkv.reshape(b, s, -1), gates.reshape(b, s, -1), conv_w, a_log, dt_bias)
    o_f, o_r = _gdn_scan(qkv, gact)
    o_b = _dilated(bqkv.reshape(b, s, -1))
    lam_params = jnp.zeros((8, LANES), F32).at[0:4, :HEAD_DIM].set(
        jnp.stack([lq1, lk1, lq2, lk2]).astype(F32))
    o_c = _diff_attn(cq.reshape(b, s, -1), ck.reshape(b, s, -1), cv.reshape(b, s, -1),
                     lam_params, subln_w, lambda_init)
    t = b * s
    x2d = _outproj(x2d, o_f.reshape(t, -1), o_r.reshape(t, -1), az, o_b.reshape(t, -1),
                   o_c.reshape(t, -1), gdn_norm_w, w_out.astype(BF16))
    return _ffn(x2d, ffn_norm_w, w_gate.astype(BF16), w_up.astype(BF16), w_down.astype(BF16),
                final_w, last)


def kernel(x, positions, attn_norm_w, w_in, conv_w, a_log, dt_bias, gdn_norm_w, lambda_q1, lambda_k1,
           lambda_q2, lambda_k2, subln_w, w_out, ffn_norm_w, w_gate, w_up, w_down, final_norm_w):
    b, s, d = x.shape
    depth = w_in.shape[0]
    cos, sin = _rope_tables(positions)
    x2d = x.reshape(b * s, d)
    for l in range(depth):
        lambda_init = 0.8 - 0.6 * math.exp(-0.3 * l)
        x2d = _layer(x2d, b, s, cos, sin, attn_norm_w[l], w_in[l], conv_w[l], a_log[l], dt_bias[l],
                     gdn_norm_w[l], lambda_q1[l], lambda_k1[l], lambda_q2[l], lambda_k2[l], subln_w[l],
                     w_out[l], ffn_norm_w[l], w_gate[l], w_up[l], w_down[l], final_norm_w,
                     lambda_init, l == depth - 1)
    return x2d.reshape(b, s, d)
```
